```python
import math
import jax, jax.numpy as jnp
from jax import lax
import numpy as np

D_MODEL = 1024
BATCH = 4
SEQ = 8192
DEPTH = 2

GRID_W = 64
HEAD_DIM = 64
NA_HEADS = 6
NA_WIN_H = 8
NA_WIN_W = 16
NA_W = NA_HEADS * HEAD_DIM
GQA_HEADS = 6
GQA_KV_HEADS = 2
GQA_QW = GQA_HEADS * HEAD_DIM
GQA_KVW = GQA_KV_HEADS * HEAD_DIM
MLA_HEADS = 4
MLA_Q_RANK = 384
MLA_KV_RANK = 256
MLA_NOPE = 64
MLA_ROPE = 32
MLA_V = 64
MLA_QK = MLA_NOPE + MLA_ROPE
MLA_W = MLA_HEADS * MLA_V
ROPE_THETA = 10000.0
Q_BLOCK = 128
N_BRANCH = 3
N_EXPERTS = 32
TOP_K = 4
D_EXPERT = 1024
SWIGLU_LIMIT = 7.0
SWIGLU_ALPHA = 1.702
MOE_BLOCK = 512
LN_EPS = 1e-5
RMS_EPS = 1e-6
DN_ALPHA = (2.0 * DEPTH) ** 0.25
DN_BETA = (8.0 * DEPTH) ** -0.25
NEG_INF = -1e30

SPLIT_WIDTHS = [NA_W, NA_W, NA_W, GQA_QW, GQA_KVW, GQA_KVW,
                MLA_Q_RANK, MLA_KV_RANK, MLA_ROPE, N_BRANCH * D_MODEL]
SPLIT_POINTS = [int(v) for v in np.cumsum(SPLIT_WIDTHS)[:-1]]
D_IN = int(sum(SPLIT_WIDTHS))

kernel_name = "hybrid_na_gqa_mla_moe_encoder"


def layer_norm(x, g, b):
    xf = x.astype(jnp.float32)
    mu = jnp.mean(xf, -1, keepdims=True)
    var = jnp.mean(jnp.square(xf - mu), -1, keepdims=True)
    y = (xf - mu) * lax.rsqrt(var + LN_EPS)
    return (y * g.astype(jnp.float32) + b.astype(jnp.float32)).astype(x.dtype)


def rms_norm(x, g):
    xf = x.astype(jnp.float32)
    y = xf * lax.rsqrt(jnp.mean(xf * xf, -1, keepdims=True) + RMS_EPS)
    return (y * g.astype(jnp.float32)).astype(x.dtype)


def axial_rope(seq_len, dim):
    quarter = dim // 4
    inv = ROPE_THETA ** (-jnp.arange(quarter, dtype=jnp.float32) / quarter)
    t = jnp.arange(seq_len)
    row = (t // GRID_W).astype(jnp.float32)
    col = (t % GRID_W).astype(jnp.float32)
    ang = jnp.concatenate([row[:, None] * inv, col[:, None] * inv], -1)
    return jnp.cos(ang), jnp.sin(ang)


def apply_rope(x, cos, sin):
    half = x.shape[-1] // 2
    xf = x.astype(jnp.float32)
    x1, x2 = xf[..., :half], xf[..., half:]
    c, s = cos[:, None, :], sin[:, None, :]
    return jnp.concatenate([x1 * c - x2 * s, x1 * s + x2 * c], -1).astype(x.dtype)


def blocked_attention(q, k, v, scale):
    B, S, H, dq = q.shape
    G = k.shape[2]
    rep = H // G
    nb = S // Q_BLOCK
    qb = jnp.moveaxis(q.reshape(B, nb, Q_BLOCK, G, rep, dq), 1, 0)

    def one_block(qblk):
        s = jnp.einsum('bqgrd,bkgd->bgrqk', qblk, k,
                       preferred_element_type=jnp.float32) * scale
        p = jax.nn.softmax(s, axis=-1)
        return jnp.einsum('bgrqk,bkgd->bqgrd', p.astype(v.dtype), v)

    out = lax.map(one_block, qb)
    return jnp.moveaxis(out, 0, 1).reshape(B, S, H, v.shape[-1])


def neighbourhood_attention(q, k, v, rpb):
    B, S, H, d = q.shape
    R = S // GRID_W
    kh = min(NA_WIN_H, R)
    kw = NA_WIN_W
    rows = jnp.arange(R)
    cols = jnp.arange(GRID_W)
    r0 = jnp.clip(rows - kh // 2, 0, R - kh)
    c0 = jnp.clip(cols - kw // 2, 0, GRID_W - kw)
    row_idx = r0[:, None] + jnp.arange(kh)[None, :]
    qg = q.reshape(B, R, GRID_W, H, d)
    kg = k.reshape(B, R, GRID_W, H, d)[:, row_idx]
    vg = v.reshape(B, R, GRID_W, H, d)[:, row_idx]
    s = jnp.einsum('brqhd,brikhd->brhqik', qg, kg,
                   preferred_element_type=jnp.float32) * (d ** -0.5)
    idx_r = row_idx - rows[:, None] + (NA_WIN_H - 1)
    dc = cols[None, :] - cols[:, None]
    idx_c = jnp.clip(dc + (NA_WIN_W - 1), 0, 2 * NA_WIN_W - 2)
    bias = rpb[:, idx_r][..., idx_c]
    bias = bias.transpose(1, 0, 3, 2, 4).astype(jnp.float32)
    in_win = (cols[None, :] >= c0[:, None]) & (cols[None, :] < c0[:, None] + kw)
    s = jnp.where(in_win[:, None, :], s + bias[None], NEG_INF)
    p = jax.nn.softmax(s.reshape(B, R, H, GRID_W, kh * GRID_W), axis=-1).reshape(s.shape)
    out = jnp.einsum('brhqik,brikhd->brqhd', p.astype(v.dtype), vg)
    return out.reshape(B, S, H, d)


def moe_ffn(x, w_router, b_router, w_gate_up, b_gate_up, w_down, b_down):
    B, S, D = x.shape
    T = B * S
    TK = T * TOP_K
    xt = x.reshape(T, D)
    logits = jnp.dot(xt, w_router, preferred_element_type=jnp.float32) + b_router.astype(jnp.float32)
    top_val, top_idx = lax.top_k(logits, TOP_K)
    gate = jax.nn.softmax(top_val, axis=-1)
    flat_e = top_idx.reshape(-1).astype(jnp.int32)
    order = jnp.argsort(flat_e)
    e_sorted = flat_e[order]
    counts = jnp.bincount(flat_e, length=N_EXPERTS).astype(jnp.int32)
    padded = ((counts + MOE_BLOCK - 1) // MOE_BLOCK) * MOE_BLOCK
    cum_padded = jnp.cumsum(padded)
    starts_sorted = jnp.cumsum(counts) - counts
    starts_padded = cum_padded - padded
    rank = jnp.arange(TK, dtype=jnp.int32) - starts_sorted[e_sorted]
    dest = starts_padded[e_sorted] + rank
    n_blocks = TK // MOE_BLOCK + N_EXPERTS
    P = n_blocks * MOE_BLOCK
    slot_token = jnp.full((P,), T, jnp.int32).at[dest].set((order // TOP_K).astype(jnp.int32))
    slot_gate = jnp.zeros((P,), jnp.float32).at[dest].set(gate.reshape(-1)[order])
    block_start = jnp.arange(n_blocks, dtype=jnp.int32) * MOE_BLOCK
    block_expert = jnp.minimum(jnp.searchsorted(cum_padded, block_start, side='right'),
                               N_EXPERTS - 1).astype(jnp.int32)
    xpad = jnp.concatenate([xt, jnp.zeros((1, D), xt.dtype)], 0)
    xb = xpad[slot_token].reshape(n_blocks, MOE_BLOCK, D)

    def expert_block(args):
        xblk, e = args
        h = jnp.dot(xblk, w_gate_up[e]) + b_gate_up[e]
        g, u = h[:, :D_EXPERT], h[:, D_EXPERT:]
        g = jnp.minimum(g, SWIGLU_LIMIT)
        u = jnp.clip(u, -SWIGLU_LIMIT, SWIGLU_LIMIT)
        a = g * jax.nn.sigmoid(SWIGLU_ALPHA * g) * (u + 1.0)
        return jnp.dot(a, w_down[e]) + b_down[e]

    yb = lax.map(expert_block, (xb, block_expert))
    y = yb.reshape(P, D).astype(jnp.float32) * slot_gate[:, None]
    out = jnp.zeros((T + 1, D), jnp.float32).at[slot_token].add(y)[:T]
    return out.reshape(B, S, D).astype(x.dtype)


def hybrid_layer(x, cos64, sin64, cos32, sin32, w_in, na_rpb, gqa_q_norm, gqa_k_norm,
                 mla_q_norm, mla_kv_norm, w_uq, w_ukv, w_branch_a, w_branch_b, w_branch_c,
                 w_out, ln1_g, ln1_b, w_router, b_router, w_gate_up, b_gate_up,
                 w_down, b_down, ln2_g, ln2_b):
    B, S, D = x.shape
    h = x @ w_in
    (na_q, na_k, na_v, g_q, g_k, g_v, c_q, c_kv, k_rope, gates) = jnp.split(h, SPLIT_POINTS, axis=-1)

    hd = (B, S, NA_HEADS, HEAD_DIM)
    y_a = neighbourhood_attention(na_q.reshape(hd), na_k.reshape(hd), na_v.reshape(hd), na_rpb)
    y_a = y_a.reshape(B, S, NA_W) @ w_branch_a

    q_b = rms_norm(g_q.reshape(B, S, GQA_HEADS, HEAD_DIM), gqa_q_norm)
    k_b = rms_norm(g_k.reshape(B, S, GQA_KV_HEADS, HEAD_DIM), gqa_k_norm)
    q_b = apply_rope(q_b, cos64, sin64)
    k_b = apply_rope(k_b, cos64, sin64)
    v_b = g_v.reshape(B, S, GQA_KV_HEADS, HEAD_DIM)
    y_b = blocked_attention(q_b, k_b, v_b, HEAD_DIM ** -0.5)
    y_b = y_b.reshape(B, S, GQA_QW) @ w_branch_b

    q_c = (rms_norm(c_q, mla_q_norm) @ w_uq).reshape(B, S, MLA_HEADS, MLA_QK)
    q_nope, q_rope = q_c[..., :MLA_NOPE], apply_rope(q_c[..., MLA_NOPE:], cos32, sin32)
    kv_c = (rms_norm(c_kv, mla_kv_norm) @ w_ukv).reshape(B, S, MLA_HEADS, MLA_NOPE + MLA_V)
    k_nope, v_c = kv_c[..., :MLA_NOPE], kv_c[..., MLA_NOPE:]
    k_r = apply_rope(k_rope.reshape(B, S, 1, MLA_ROPE), cos32, sin32)
    q_full = jnp.concatenate([q_nope, q_rope], -1)
    k_full = jnp.concatenate([k_nope, jnp.broadcast_to(k_r, (B, S, MLA_HEADS, MLA_ROPE))], -1)
    y_c = blocked_attention(q_full, k_full, v_c, MLA_QK ** -0.5)
    y_c = y_c.reshape(B, S, MLA_W) @ w_branch_c

    g = jax.nn.sigmoid(gates.astype(jnp.float32)).reshape(B, S, N_BRANCH, D).astype(x.dtype)
    mixed = g[:, :, 0] * y_a + g[:, :, 1] * y_b + g[:, :, 2] * y_c
    x = layer_norm(DN_ALPHA * x + mixed @ w_out, ln1_g, ln1_b)

    f = moe_ffn(x, w_router, b_router, w_gate_up, b_gate_up, w_down, b_down)
    return layer_norm(DN_ALPHA * x + f, ln2_g, ln2_b)


def setup_inputs(seed: int = 0) -> dict:
    key = jax.random.key(seed)
    ks = jax.random.split(key, 28)
    L, D, E, F = DEPTH, D_MODEL, N_EXPERTS, D_EXPERT

    def nrm(k, shape, scale):
        return jax.random.normal(k, shape, jnp.float32) * scale

    return {
        "x": nrm(ks[0], (BATCH, SEQ, D), 1.0),
        "w_in": nrm(ks[1], (L, D, D_IN), D ** -0.5),
        "na_rpb": nrm(ks[2], (L, NA_HEADS, 2 * NA_WIN_H - 1, 2 * NA_WIN_W - 1), 0.1),
        "gqa_q_norm": 1.0 + nrm(ks[3], (L, HEAD_DIM), 0.02),
        "gqa_k_norm": 1.0 + nrm(ks[4], (L, HEAD_DIM), 0.02),
        "mla_q_norm": 1.0 + nrm(ks[5], (L, MLA_Q_RANK), 0.02),
        "mla_kv_norm": 1.0 + nrm(ks[6], (L, MLA_KV_RANK), 0.02),
        "w_uq": nrm(ks[7], (L, MLA_Q_RANK, MLA_HEADS * MLA_QK), MLA_Q_RANK ** -0.5),
        "w_ukv": nrm(ks[8], (L, MLA_KV_RANK, MLA_HEADS * (MLA_NOPE + MLA_V)), MLA_KV_RANK ** -0.5),
        "w_branch_a": nrm(ks[9], (L, NA_W, D), NA_W ** -0.5),
        "w_branch_b": nrm(ks[10], (L, GQA_QW, D), GQA_QW ** -0.5),
        "w_branch_c": nrm(ks[11], (L, MLA_W, D), MLA_W ** -0.5),
        "w_out": nrm(ks[12], (L, D, D), DN_BETA * D ** -0.5),
        "ln1_g": 1.0 + nrm(ks[13], (L, D), 0.02),
        "ln1_b": nrm(ks[14], (L, D), 0.02),
        "w_router": nrm(ks[15], (L, D, E), D ** -0.5),
        "b_router": nrm(ks[16], (L, E), 0.01),
        "w_gate_up": nrm(ks[17], (L, E, D, 2 * F), D ** -0.5),
        "b_gate_up": nrm(ks[18], (L, E, 2 * F), 0.01),
        "w_down": nrm(ks[19], (L, E, F, D), DN_BETA * F ** -0.5),
        "b_down": nrm(ks[20], (L, E, D), 0.01),
        "ln2_g": 1.0 + nrm(ks[21], (L, D), 0.02),
        "ln2_b": nrm(ks[22], (L, D), 0.02),
    }


def reference(x, w_in, na_rpb, gqa_q_norm, gqa_k_norm, mla_q_norm, mla_kv_norm, w_uq, w_ukv,
              w_branch_a, w_branch_b, w_branch_c, w_out, ln1_g, ln1_b, w_router, b_router,
              w_gate_up, b_gate_up, w_down, b_down, ln2_g, ln2_b):
    S = x.shape[1]
    cos64, sin64 = axial_rope(S, HEAD_DIM)
    cos32, sin32 = axial_rope(S, MLA_ROPE)
    for l in range(DEPTH):
        x = hybrid_layer(x, cos64, sin64, cos32, sin32, w_in[l], na_rpb[l], gqa_q_norm[l],
                         gqa_k_norm[l], mla_q_norm[l], mla_kv_norm[l], w_uq[l], w_ukv[l],
                         w_branch_a[l], w_branch_b[l], w_branch_c[l], w_out[l], ln1_g[l],
                         ln1_b[l], w_router[l], b_router[l], w_gate_up[l], b_gate_up[l],
                         w_down[l], b_down[l], ln2_g[l], ln2_b[l])
    return x
```

```python
import functools
import math

import jax
import jax.numpy as jnp
from jax import lax
from jax.experimental import pallas as pl
from jax.experimental.pallas import tpu as pltpu

F32 = jnp.float32
BF16 = jnp.bfloat16

LANE = 128
GRID_W = 64
HEAD_DIM = 64
NA_HEADS = 6
NA_WIN_H = 8
NA_WIN_W = 16
NA_W = NA_HEADS * HEAD_DIM
GQA_HEADS = 6
GQA_KV_HEADS = 2
GQA_REP = GQA_HEADS // GQA_KV_HEADS
MLA_HEADS = 4
MLA_Q_RANK = 384
MLA_KV_RANK = 256
MLA_NOPE = 64
MLA_ROPE = 32
MLA_V = 64
MLA_QK = MLA_NOPE + MLA_ROPE
ROPE_THETA = 10000.0
N_BRANCH = 3
N_EXPERTS = 32
TOP_K = 4
D_EXPERT = 1024
SWIGLU_LIMIT = 7.0
SWIGLU_ALPHA = 1.702
MOE_BLOCK = 512
LN_EPS = 1e-5
RMS_EPS = 1e-6
NEG_INF = -1e30
VMEM_LIMIT = 56 * 1024 * 1024

_C_NAQ = 0
_C_NAK = _C_NAQ + NA_W
_C_NAV = _C_NAK + NA_W
_C_GQ = _C_NAV + NA_W
_C_GK = _C_GQ + GQA_HEADS * LANE
_C_GV = _C_GK + GQA_KV_HEADS * LANE
_C_CQ = _C_GV + GQA_KV_HEADS * LANE
_C_CKV = _C_CQ + MLA_Q_RANK
_C_KR = _C_CKV + MLA_KV_RANK
_C_END = _C_KR + LANE


def _cparams(sem):
    return pltpu.CompilerParams(dimension_semantics=sem, vmem_limit_bytes=VMEM_LIMIT)


def _dot(a, b):
    return jnp.dot(a, b, preferred_element_type=F32)


def _dot_nt(a, b):
    return lax.dot_general(a, b, (((1,), (1,)), ((), ())), preferred_element_type=F32)


def _qkv_kernel(x_ref, w_ref, cs_ref, mq_ref, mk_ref, gq_ref, gk_ref, nq_ref, nkv_ref,
                wuq_ref, wuk_ref, wuv_ref,
                naq_ref, nak_ref, nav_ref, gqo_ref, gko_ref, gvo_ref, mqo_ref, mko_ref, mvo_ref):
    tm = x_ref.shape[0]
    xb = x_ref[...].astype(BF16)

    def proj(c0, width):
        return _dot(xb, w_ref[:, c0:c0 + width])

    naq_ref[...] = proj(_C_NAQ, NA_W).astype(BF16)
    nak_ref[...] = proj(_C_NAK, NA_W).astype(BF16)
    nav_ref[...] = proj(_C_NAV, NA_W).astype(BF16)

    lane = lax.broadcasted_iota(jnp.int32, (tm, LANE), 1)
    cs = cs_ref[...]

    def norm_rope(hc, gain):
        r = lax.rsqrt(jnp.mean(hc * hc, axis=-1, keepdims=True) + RMS_EPS)
        a = hc * r * (gain * cs)
        return jnp.where(lane < HEAD_DIM, a + pltpu.roll(a, HEAD_DIM, 1), 0.0)

    gq_gain = gq_ref[...]
    for h in range(GQA_HEADS):
        hc = proj(_C_GQ + h * LANE, LANE)
        gqo_ref[:, h * LANE:(h + 1) * LANE] = norm_rope(hc, gq_gain).astype(BF16)
    gk_gain = gk_ref[...]
    for g in range(GQA_KV_HEADS):
        hc = proj(_C_GK + g * LANE, LANE)
        gko_ref[:, g * LANE:(g + 1) * LANE] = norm_rope(hc, gk_gain).astype(BF16)
        hv = proj(_C_GV + g * LANE, LANE)
        gvo_ref[:, g * LANE:(g + 1) * LANE] = jnp.where(lane == HEAD_DIM, 1.0, hv).astype(BF16)

    def rms(v, gain):
        r = lax.rsqrt(jnp.mean(v * v, axis=-1, keepdims=True) + RMS_EPS)
        return (v * r * gain).astype(BF16)

    def mla_rope(b):
        summed = b + pltpu.roll(b, LANE - MLA_ROPE, 1)
        return jnp.where(lane < MLA_NOPE, b, jnp.where(lane < MLA_QK, summed, 0.0))

    hq = _dot(rms(proj(_C_CQ, MLA_Q_RANK), nq_ref[...]), wuq_ref[...])
    mq = mq_ref[...]
    for h in range(MLA_HEADS):
        mqo_ref[:, h * LANE:(h + 1) * LANE] = mla_rope(hq[:, h * LANE:(h + 1) * LANE] * mq).astype(BF16)

    ckv = rms(proj(_C_CKV, MLA_KV_RANK), nkv_ref[...])
    hk = _dot(ckv, wuk_ref[...])
    hv = _dot(ckv, wuv_ref[...])
    kr = mla_rope(proj(_C_KR, LANE) * mk_ref[...])
    for h in range(MLA_HEADS):
        sl = slice(h * LANE, (h + 1) * LANE)
        mko_ref[:, sl] = (hk[:, sl] + kr).astype(BF16)
        mvo_ref[:, sl] = jnp.where(lane == MLA_V, 1.0, hv[:, sl]).astype(BF16)


def _qkv_call(x2, wqkv, cs, mq, mk, gq, gk, nq, nkv, wuq, wuk, wuv, *, seq, tm):
    T, D = x2.shape
    ns = seq // tm

    def rows(width):
        return pl.BlockSpec((tm, width), lambda i: (i, 0))

    def pos(width):
        return pl.BlockSpec((tm, width), lambda i: (i % ns, 0))

    def whole(a):
        return pl.BlockSpec(a.shape, lambda i: (0,) * a.ndim)

    widths = [NA_W, NA_W, NA_W, GQA_HEADS * LANE, GQA_KV_HEADS * LANE, GQA_KV_HEADS * LANE,
              MLA_HEADS * LANE, MLA_HEADS * LANE, MLA_HEADS * LANE]
    return pl.pallas_call(
        _qkv_kernel,
        grid=(T // tm,),
        in_specs=[rows(D), whole(wqkv), pos(LANE), pos(LANE), pos(LANE), whole(gq), whole(gk),
                  whole(nq), whole(nkv), whole(wuq), whole(wuk), whole(wuv)],
        out_specs=[rows(w) for w in widths],
        out_shape=[jax.ShapeDtypeStruct((T, w), BF16) for w in widths],
        compiler_params=_cparams(("parallel",)),
        name="qkv_proj",
    )(x2, wqkv, cs, mq, mk, gq, gk, nq, nkv, wuq, wuk, wuv)


NA_ROWS_PER_STEP = 8


def _na_kernel(q_ref, k_ref, v_ref, bias_ref, o_ref, *, n_rows):
    j = pl.program_id(1)
    win = NA_WIN_H * GRID_W
    lane = lax.broadcasted_iota(jnp.int32, (GRID_W, LANE), 1)
    low = lane < HEAD_DIM

    def row_body(a, carry):
        r = j * NA_ROWS_PER_STEP + a
        r0 = jnp.clip(r - NA_WIN_H // 2, 0, n_rows - NA_WIN_H)
        variant = r - r0
        qrow = pl.ds(pl.multiple_of(a * GRID_W, GRID_W), GRID_W)
        krow = pl.ds(pl.multiple_of(r0 * GRID_W, GRID_W), win)
        for pair in range(NA_HEADS // 2):
            cols = slice(pair * LANE, (pair + 1) * LANE)
            qp = q_ref[qrow, cols]
            kp = k_ref[krow, cols]
            vp = v_ref[krow, cols]
            outs = []
            for half in range(2):
                qm = jnp.where(low if half == 0 else jnp.logical_not(low), qp, jnp.zeros_like(qp))
                s = _dot_nt(qm, kp) + bias_ref[variant, 2 * pair + half]
                m = jnp.max(s, axis=-1, keepdims=True)
                p = jnp.exp(s - m)
                l = jnp.sum(p, axis=-1, keepdims=True)
                outs.append(_dot(p.astype(BF16), vp) / l)
            o_ref[qrow, cols] = jnp.where(low, outs[0], outs[1]).astype(BF16)
        return carry

    lax.fori_loop(0, NA_ROWS_PER_STEP, row_body, 0)


def _na_call(q, k, v, bias, *, batch, seq):
    n_rows = seq // GRID_W
    steps = n_rows // NA_ROWS_PER_STEP
    tq = NA_ROWS_PER_STEP * GRID_W
    return pl.pallas_call(
        functools.partial(_na_kernel, n_rows=n_rows),
        grid=(batch, steps),
        in_specs=[pl.BlockSpec((tq, NA_W), lambda b, j: (b * steps + j, 0)),
                  pl.BlockSpec((seq, NA_W), lambda b, j: (b, 0)),
                  pl.BlockSpec((seq, NA_W), lambda b, j: (b, 0)),
                  pl.BlockSpec(bias.shape, lambda b, j: (0, 0, 0, 0))],
        out_specs=pl.BlockSpec((tq, NA_W), lambda b, j: (b * steps + j, 0)),
        out_shape=jax.ShapeDtypeStruct(q.shape, BF16),
        compiler_params=_cparams(("parallel", "arbitrary")),
        name="na_attn",
    )(q, k, v, bias)


def _na_bias_table(rpb):
    cols = jnp.arange(GRID_W)
    c0 = jnp.clip(cols - NA_WIN_W // 2, 0, GRID_W - NA_WIN_W)
    in_win = (cols[None, :] >= c0[:, None]) & (cols[None, :] < c0[:, None] + NA_WIN_W)
    idx_c = jnp.clip(cols[None, :] - cols[:, None] + (NA_WIN_W - 1), 0, 2 * NA_WIN_W - 2)
    variant = jnp.arange(NA_WIN_H)
    idx_r = jnp.arange(NA_WIN_H)[None, :] - variant[:, None] + (NA_WIN_H - 1)
    b = rpb.astype(F32)[:, idx_r]
    b = b[..., idx_c]
    b = jnp.where(in_win[None, None, None], b, NEG_INF)
    b = b.transpose(1, 0, 3, 2, 4)
    return b.reshape(NA_WIN_H, NA_HEADS, GRID_W, NA_WIN_H * GRID_W)


def _flash_kernel(q_ref, k_ref, v_ref, o_ref, m_scr, acc_scr, *, rep, tk, nk, sum_lane):
    m_scr[...] = jnp.full(m_scr.shape, NEG_INF, F32)
    acc_scr[...] = jnp.zeros(acc_scr.shape, F32)

    def body(c, carry):
        rows = pl.ds(pl.multiple_of(c * tk, tk), tk)
        k = k_ref[rows, :]
        v = v_ref[rows, :]
        for r in range(rep):
            q = q_ref[:, r * LANE:(r + 1) * LANE]
            s = _dot_nt(q, k)
            m_prev = m_scr[r]
            m_new = jnp.maximum(m_prev, jnp.max(s, axis=-1, keepdims=True))
            alpha = jnp.exp(m_prev - m_new)
            p = jnp.exp(s - m_new[:, :1])
            acc_scr[r] = alpha * acc_scr[r] + _dot(p.astype(BF16), v)
            m_scr[r] = m_new
        return carry

    lax.fori_loop(0, nk, body, 0)
    for r in range(rep):
        acc = acc_scr[r]
        o_ref[:, r * LANE:(r + 1) * LANE] = (acc / acc[:, sum_lane:sum_lane + 1]).astype(BF16)


def _flash_call(q, k, v, *, batch, seq, kv_heads, rep, tq, tk, sum_lane, name):
    nq = seq // tq
    width = rep * LANE
    return pl.pallas_call(
        functools.partial(_flash_kernel, rep=rep, tk=tk, nk=seq // tk, sum_lane=sum_lane),
        grid=(batch, kv_heads, nq),
        in_specs=[pl.BlockSpec((tq, width), lambda b, g, i: (b * nq + i, g)),
                  pl.BlockSpec((seq, LANE), lambda b, g, i: (b, g)),
                  pl.BlockSpec((seq, LANE), lambda b, g, i: (b, g))],
        out_specs=pl.BlockSpec((tq, width), lambda b, g, i: (b * nq + i, g)),
        out_shape=jax.ShapeDtypeStruct(q.shape, BF16),
        scratch_shapes=[pltpu.VMEM((rep, tq, LANE), F32), pltpu.VMEM((rep, tq, LANE), F32)],
        compiler_params=_cparams(("parallel", "parallel", "arbitrary")),
        name=name,
    )(q, k, v)


def _layer_norm(z, g, b):
    mu = jnp.mean(z, axis=-1, keepdims=True)
    zc = z - mu
    var = jnp.mean(zc * zc, axis=-1, keepdims=True)
    return zc * lax.rsqrt(var + LN_EPS) * g + b


def _merge_kernel(x_ref, oa_ref, ob_ref, oc_ref, wg_ref, wa_ref, wb_ref, wc_ref, wo_ref,
                  lng_ref, lnb_ref, wrh_ref, wrl_ref, br_ref, x1_ref, x1b_ref, logit_ref, *, alpha):
    d = x_ref.shape[1]
    x = x_ref[...]
    xb = x.astype(BF16)
    mixed = None
    for i, (o_ref, w_ref) in enumerate(((oa_ref, wa_ref), (ob_ref, wb_ref), (oc_ref, wc_ref))):
        gate = jax.nn.sigmoid(_dot(xb, wg_ref[:, i * d:(i + 1) * d]))
        term = gate * _dot(o_ref[...], w_ref[...])
        mixed = term if mixed is None else mixed + term
    z = alpha * x + _dot(mixed.astype(BF16), wo_ref[...])
    x1 = _layer_norm(z, lng_ref[...], lnb_ref[...])
    x1_ref[...] = x1
    hi = x1.astype(BF16)
    lo = (x1 - hi.astype(F32)).astype(BF16)
    x1b_ref[...] = hi
    logit_ref[...] = (_dot(hi, wrh_ref[...]) + _dot(lo, wrh_ref[...]) + _dot(hi, wrl_ref[...])
                      + br_ref[...])


def _merge_call(x2, oa, ob, oc, wg, wa, wb, wc, wo, lng, lnb, wrh, wrl, br, *, alpha, tm):
    T, D = x2.shape

    def rows(width):
        return pl.BlockSpec((tm, width), lambda i: (i, 0))

    def whole(a):
        return pl.BlockSpec(a.shape, lambda i: (0,) * a.ndim)

    return pl.pallas_call(
        functools.partial(_merge_kernel, alpha=alpha),
        grid=(T // tm,),
        in_specs=[rows(D), rows(oa.shape[1]), rows(ob.shape[1]), rows(oc.shape[1]),
                  whole(wg), whole(wa), whole(wb), whole(wc), whole(wo),
                  whole(lng), whole(lnb), whole(wrh), whole(wrl), whole(br)],
        out_specs=[rows(D), rows(D), rows(LANE)],
        out_shape=[jax.ShapeDtypeStruct((T, D), F32), jax.ShapeDtypeStruct((T, D), BF16),
                   jax.ShapeDtypeStruct((T, LANE), F32)],
        compiler_params=_cparams(("parallel",)),
        name="merge_ln_router",
    )(x2, oa, ob, oc, wg, wa, wb, wc, wo, lng, lnb, wrh, wrl, br)


def _moe_kernel(be_ref, na_ref, xs_ref, wgu_ref, bgu_ref, wd_ref, bd_ref, y_ref):
    @pl.when(pl.program_id(0) < na_ref[0])
    def _():
        h = _dot(xs_ref[...], wgu_ref[0]) + bgu_ref[0]
        g = jnp.minimum(h[:, :D_EXPERT], SWIGLU_LIMIT)
        u = jnp.clip(h[:, D_EXPERT:], -SWIGLU_LIMIT, SWIGLU_LIMIT)
        a = g * jax.nn.sigmoid(SWIGLU_ALPHA * g) * (u + 1.0)
        y_ref[...] = _dot(a.astype(BF16), wd_ref[0]) + bd_ref[0]


def _moe_call(block_expert, n_active, xs, wgu, bgu, wd, bd):
    P, D = xs.shape
    n_blocks = P // MOE_BLOCK

    def blk(i, be, na):
        return (jnp.minimum(i, na[0] - 1), 0)

    def per_expert(i, be, na):
        return (be[jnp.minimum(i, na[0] - 1)], 0, 0)

    grid_spec = pltpu.PrefetchScalarGridSpec(
        num_scalar_prefetch=2,
        grid=(n_blocks,),
        in_specs=[pl.BlockSpec((MOE_BLOCK, D), blk),
                  pl.BlockSpec((1, D, 2 * D_EXPERT), per_expert),
                  pl.BlockSpec((1, 1, 2 * D_EXPERT), per_expert),
                  pl.BlockSpec((1, D_EXPERT, D), per_expert),
                  pl.BlockSpec((1, 1, D), per_expert)],
        out_specs=pl.BlockSpec((MOE_BLOCK, D), blk),
    )
    return pl.pallas_call(
        _moe_kernel,
        grid_spec=grid_spec,
        out_shape=jax.ShapeDtypeStruct((P, D), F32),
        compiler_params=_cparams(("arbitrary",)),
        name="moe_ffn",
    )(block_expert, n_active, xs, wgu, bgu, wd, bd)


def _final_kernel(x_ref, y_ref, gate_ref, lng_ref, lnb_ref, o_ref, *, alpha):
    d = x_ref.shape[1]
    gate = gate_ref[...]
    f = None
    for k in range(TOP_K):
        term = y_ref[:, k * d:(k + 1) * d] * gate[:, k:k + 1]
        f = term if f is None else f + term
    o_ref[...] = _layer_norm(alpha * x_ref[...] + f, lng_ref[...], lnb_ref[...])


def _final_call(x1, yk, gate, lng, lnb, *, alpha, tm):
    T, D = x1.shape
    return pl.pallas_call(
        functools.partial(_final_kernel, alpha=alpha),
        grid=(T // tm,),
        in_specs=[pl.BlockSpec((tm, D), lambda i: (i, 0)),
                  pl.BlockSpec((tm, TOP_K * D), lambda i: (i, 0)),
                  pl.BlockSpec((tm, TOP_K), lambda i: (i, 0)),
                  pl.BlockSpec((1, D), lambda i: (0, 0)),
                  pl.BlockSpec((1, D), lambda i: (0, 0))],
        out_specs=pl.BlockSpec((tm, D), lambda i: (i, 0)),
        out_shape=jax.ShapeDtypeStruct((T, D), F32),
        compiler_params=_cparams(("parallel",)),
        name="combine_ln",
    )(x1, yk, gate, lng, lnb)


def _rot_half(w):
    half = w.shape[-1] // 2
    return jnp.concatenate([w[..., half:], w[..., :half]], axis=-1)


def _axial_tables(seq):
    def cos_sin(dim):
        quarter = dim // 4
        inv = ROPE_THETA ** (-jnp.arange(quarter, dtype=F32) / quarter)
        t = jnp.arange(seq)
        row = (t // GRID_W).astype(F32)
        col = (t % GRID_W).astype(F32)
        ang = jnp.concatenate([row[:, None] * inv, col[:, None] * inv], -1)
        return jnp.cos(ang), jnp.sin(ang)

    c64, s64 = cos_sin(HEAD_DIM)
    c32, s32 = cos_sin(MLA_ROPE)
    cs = jnp.concatenate([c64, c64, -s64, s64], -1)
    m = jnp.concatenate([jnp.ones((seq, MLA_NOPE), F32), c32, c32, -s32, s32], -1)
    return cs, m * (MLA_QK ** -0.5), m


def _layer_weights(w_in, gqa_q_norm, gqa_k_norm, w_uq, w_ukv, w_branch_b, w_branch_c):
    D = w_in.shape[0]
    widths = [NA_W, NA_W, NA_W, GQA_HEADS * HEAD_DIM, GQA_KV_HEADS * HEAD_DIM, GQA_KV_HEADS * HEAD_DIM,
              MLA_Q_RANK, MLA_KV_RANK, MLA_ROPE, N_BRANCH * D]
    offs = [0]
    for w in widths:
        offs.append(offs[-1] + w)
    na_q, na_k, na_v, g_q, g_k, g_v, c_q, c_kv, k_r, gates = [w_in[:, offs[i]:offs[i + 1]] for i in range(10)]

    def heads_with_rot(w, n):
        w = w.reshape(D, n, HEAD_DIM)
        return jnp.concatenate([w, _rot_half(w)], -1).reshape(D, n * LANE)

    def heads_padded(w, n, width):
        w = w.reshape(w.shape[0], n, width)
        return jnp.pad(w, ((0, 0), (0, 0), (0, LANE - width))).reshape(w.shape[0], n * LANE)

    kr_cols = jnp.concatenate([jnp.zeros((D, MLA_NOPE), F32), k_r, _rot_half(k_r)], -1)
    wqkv = jnp.concatenate([na_q * (HEAD_DIM ** -0.5), na_k, na_v,
                            heads_with_rot(g_q, GQA_HEADS), heads_with_rot(g_k, GQA_KV_HEADS),
                            heads_padded(g_v, GQA_KV_HEADS, HEAD_DIM), c_q, c_kv, kr_cols], -1).astype(BF16)

    gq = (jnp.concatenate([gqa_q_norm, _rot_half(gqa_q_norm)]) * (HEAD_DIM ** -0.5)).reshape(1, LANE)
    gk = jnp.concatenate([gqa_k_norm, _rot_half(gqa_k_norm)]).reshape(1, LANE)

    uq = w_uq.reshape(MLA_Q_RANK, MLA_HEADS, MLA_QK)
    uq_rope = uq[..., MLA_NOPE:]
    wuq = jnp.concatenate([uq, _rot_half(uq_rope)], -1).reshape(MLA_Q_RANK, MLA_HEADS * LANE).astype(BF16)
    ukv = w_ukv.reshape(MLA_KV_RANK, MLA_HEADS, MLA_NOPE + MLA_V)
    wuk = heads_padded(ukv[..., :MLA_NOPE].reshape(MLA_KV_RANK, -1), MLA_HEADS, MLA_NOPE).astype(BF16)
    wuv = heads_padded(ukv[..., MLA_NOPE:].reshape(MLA_KV_RANK, -1), MLA_HEADS, MLA_V).astype(BF16)

    def rows_padded(w, n, width):
        w = w.reshape(n, width, D)
        return jnp.pad(w, ((0, 0), (0, LANE - width), (0, 0))).reshape(n * LANE, D).astype(BF16)

    wb = rows_padded(w_branch_b, GQA_HEADS, HEAD_DIM)
    wc = rows_padded(w_branch_c, MLA_HEADS, MLA_V)
    return wqkv, gq, gk, wuq, wuk, wuv, gates.astype(BF16), wb, wc


def _routing(logits, n_tokens):
    top_val, top_idx = lax.top_k(logits, TOP_K)
    gate = jax.nn.softmax(top_val, axis=-1)
    sel = jnp.sum(jax.nn.one_hot(top_idx, N_EXPERTS, dtype=jnp.int32), axis=1)
    incl = jnp.cumsum(sel, axis=0)
    counts = incl[-1]
    rank = incl - sel
    padded = ((counts + MOE_BLOCK - 1) // MOE_BLOCK) * MOE_BLOCK
    cum_padded = jnp.cumsum(padded)
    starts = cum_padded - padded
    dest = jnp.take_along_axis(starts[None, :] + rank, top_idx, axis=1).astype(jnp.int32)
    n_blocks = n_tokens * TOP_K // MOE_BLOCK + N_EXPERTS
    slots = n_blocks * MOE_BLOCK
    tok = jnp.broadcast_to(jnp.arange(n_tokens, dtype=jnp.int32)[:, None], dest.shape)
    slot_token = jnp.zeros((slots,), jnp.int32).at[dest.reshape(-1)].set(tok.reshape(-1))
    block_start = jnp.arange(n_blocks, dtype=jnp.int32) * MOE_BLOCK
    block_expert = jnp.minimum(jnp.searchsorted(cum_padded, block_start, side='right'),
                               N_EXPERTS - 1).astype(jnp.int32)
    n_active = (cum_padded[-1] // MOE_BLOCK).astype(jnp.int32).reshape(1)
    return gate, dest, slot_token, block_expert, n_active


def kernel(x, w_in, na_rpb, gqa_q_norm, gqa_k_norm, mla_q_norm, mla_kv_norm, w_uq, w_ukv, w_branch_a, w_branch_b, w_branch_c, w_out, ln1_g, ln1_b, w_router, b_router, w_gate_up, b_gate_up, w_down, b_down, ln2_g, ln2_b):
    B, S, D = x.shape
    T = B * S
    depth = w_in.shape[0]
    alpha = (2.0 * depth) ** 0.25
    tm = min(512, S)
    tq = min(512, S)
    cs, mq_tab, mk_tab = _axial_tables(S)
    x2 = x.reshape(T, D)
    for l in range(depth):
        wqkv, gq, gk, wuq, wuk, wuv, wg, wb, wc = _layer_weights(
            w_in[l], gqa_q_norm[l], gqa_k_norm[l], w_uq[l], w_ukv[l], w_branch_b[l], w_branch_c[l])
        naq, nak, nav, gqo, gko, gvo, mqo, mko, mvo = _qkv_call(
            x2, wqkv, cs, mq_tab, mk_tab, gq, gk, mla_q_norm[l].reshape(1, -1), mla_kv_norm[l].reshape(1, -1),
            wuq, wuk, wuv, seq=S, tm=tm)
        oa = _na_call(naq, nak, nav, _na_bias_table(na_rpb[l]), batch=B, seq=S)
        ob = _flash_call(gqo, gko, gvo, batch=B, seq=S, kv_heads=GQA_KV_HEADS, rep=GQA_REP,
                         tq=tq, tk=tq, sum_lane=HEAD_DIM, name="gqa_attn")
        oc = _flash_call(mqo, mko, mvo, batch=B, seq=S, kv_heads=MLA_HEADS, rep=1,
                         tq=tq, tk=tq, sum_lane=MLA_V, name="mla_attn")
        wr = jnp.pad(w_router[l], ((0, 0), (0, LANE - N_EXPERTS)))
        wrh = wr.astype(BF16)
        wrl = (wr - wrh.astype(F32)).astype(BF16)
        br = jnp.pad(b_router[l], (0, LANE - N_EXPERTS)).reshape(1, LANE)
        x1, x1b, logits = _merge_call(
            x2, oa, ob, oc, wg, w_branch_a[l].astype(BF16), wb, wc, w_out[l].astype(BF16),
            ln1_g[l].reshape(1, D), ln1_b[l].reshape(1, D), wrh, wrl, br, alpha=alpha, tm=tm)
        gate, dest, slot_token, block_expert, n_active = _routing(logits[:, :N_EXPERTS], T)
        xs = jnp.take(x1b, slot_token, axis=0)
        y = _moe_call(block_expert, n_active, xs, w_gate_up[l].astype(BF16),
                      b_gate_up[l].reshape(N_EXPERTS, 1, -1), w_down[l].astype(BF16),
                      b_down[l].reshape(N_EXPERTS, 1, -1))
        yk = jnp.take(y, dest.reshape(-1), axis=0).reshape(T, TOP_K * D)
        x2 = _final_call(x1, yk, gate, ln2_g[l].reshape(1, D), ln2_b[l].reshape(1, D), alpha=alpha, tm=tm)
    return x2.reshape(B, S, D)
```

```python
import functools
import math

import jax
import jax.numpy as jnp
from jax import lax
from jax.experimental import pallas as pl
from jax.experimental.pallas import tpu as pltpu

F32 = jnp.float32
BF16 = jnp.bfloat16

LANE = 128
GRID_W = 64
HEAD_DIM = 64
NA_HEADS = 6
NA_WIN_H = 8
NA_WIN_W = 16
NA_W = NA_HEADS * HEAD_DIM
GQA_HEADS = 6
GQA_KV_HEADS = 2
GQA_REP = GQA_HEADS // GQA_KV_HEADS
MLA_HEADS = 4
MLA_Q_RANK = 384
MLA_KV_RANK = 256
MLA_NOPE = 64
MLA_ROPE = 32
MLA_V = 64
MLA_QK = MLA_NOPE + MLA_ROPE
ROPE_THETA = 10000.0
N_BRANCH = 3
N_EXPERTS = 32
TOP_K = 4
D_EXPERT = 1024
SWIGLU_LIMIT = 7.0
SWIGLU_ALPHA = 1.702
MOE_BLOCK = 512
LN_EPS = 1e-5
RMS_EPS = 1e-6
NEG_INF = -1e30
LOG2_E = math.log2(math.e)
VMEM_LIMIT = 56 * 1024 * 1024

_C_NAQ = 0
_C_NAK = _C_NAQ + NA_W
_C_NAV = _C_NAK + NA_W
_C_GQ = _C_NAV + NA_W
_C_GK = _C_GQ + GQA_HEADS * LANE
_C_GV = _C_GK + GQA_KV_HEADS * LANE
_C_CQ = _C_GV + GQA_KV_HEADS * LANE
_C_CKV = _C_CQ + MLA_Q_RANK
_C_KR = _C_CKV + MLA_KV_RANK
_C_END = _C_KR + LANE


def _cparams(sem):
    return pltpu.CompilerParams(dimension_semantics=sem, vmem_limit_bytes=VMEM_LIMIT)


def _dot(a, b):
    return jnp.dot(a, b, preferred_element_type=F32)


def _dot_nt(a, b):
    return lax.dot_general(a, b, (((1,), (1,)), ((), ())), preferred_element_type=F32)


def _qkv_kernel(x_ref, w_ref, cs_ref, mq_ref, mk_ref, gq_ref, gk_ref, nq_ref, nkv_ref,
                wuq_ref, wuk_ref, wuv_ref,
                naq_ref, nak_ref, nav_ref, gqo_ref, gko_ref, gvo_ref, mqo_ref, mko_ref, mvo_ref):
    tm = x_ref.shape[0]
    xb = x_ref[...].astype(BF16)

    def proj(c0, width):
        return _dot(xb, w_ref[:, c0:c0 + width])

    naq_ref[...] = proj(_C_NAQ, NA_W).astype(BF16)
    nak_ref[...] = proj(_C_NAK, NA_W).astype(BF16)
    nav_ref[...] = proj(_C_NAV, NA_W).astype(BF16)

    lane = lax.broadcasted_iota(jnp.int32, (tm, LANE), 1)
    cs = cs_ref[...]

    def norm_rope(hc, gain):
        r = lax.rsqrt(jnp.mean(hc * hc, axis=-1, keepdims=True) + RMS_EPS)
        a = hc * r * (gain * cs)
        return jnp.where(lane < HEAD_DIM, a + pltpu.roll(a, HEAD_DIM, 1), 0.0)

    gq_gain = gq_ref[...]
    for h in range(GQA_HEADS):
        hc = proj(_C_GQ + h * LANE, LANE)
        gqo_ref[:, h * LANE:(h + 1) * LANE] = norm_rope(hc, gq_gain).astype(BF16)
    gk_gain = gk_ref[...]
    for g in range(GQA_KV_HEADS):
        hc = proj(_C_GK + g * LANE, LANE)
        gko_ref[:, g * LANE:(g + 1) * LANE] = norm_rope(hc, gk_gain).astype(BF16)
        hv = proj(_C_GV + g * LANE, LANE)
        gvo_ref[:, g * LANE:(g + 1) * LANE] = jnp.where(lane == HEAD_DIM, 1.0, hv).astype(BF16)

    def rms(v, gain):
        r = lax.rsqrt(jnp.mean(v * v, axis=-1, keepdims=True) + RMS_EPS)
        return (v * r * gain).astype(BF16)

    def mla_rope(b):
        summed = b + pltpu.roll(b, LANE - MLA_ROPE, 1)
        return jnp.where(lane < MLA_NOPE, b, jnp.where(lane < MLA_QK, summed, 0.0))

    hq = _dot(rms(proj(_C_CQ, MLA_Q_RANK), nq_ref[...]), wuq_ref[...])
    mq = mq_ref[...]
    for h in range(MLA_HEADS):
        mqo_ref[:, h * LANE:(h + 1) * LANE] = mla_rope(hq[:, h * LANE:(h + 1) * LANE] * mq).astype(BF16)

    ckv = rms(proj(_C_CKV, MLA_KV_RANK), nkv_ref[...])
    hk = _dot(ckv, wuk_ref[...])
    hv = _dot(ckv, wuv_ref[...])
    kr = mla_rope(proj(_C_KR, LANE) * mk_ref[...])
    for h in range(MLA_HEADS):
        sl = slice(h * LANE, (h + 1) * LANE)
        mko_ref[:, sl] = (hk[:, sl] + kr).astype(BF16)
        mvo_ref[:, sl] = jnp.where(lane == MLA_V, 1.0, hv[:, sl]).astype(BF16)


def _qkv_call(x2, wqkv, cs, mq, mk, gq, gk, nq, nkv, wuq, wuk, wuv, *, seq, tm):
    T, D = x2.shape
    ns = seq // tm

    def rows(width):
        return pl.BlockSpec((tm, width), lambda i: (i, 0))

    def pos(width):
        return pl.BlockSpec((tm, width), lambda i: (i % ns, 0))

    def whole(a):
        return pl.BlockSpec(a.shape, lambda i: (0,) * a.ndim)

    widths = [NA_W, NA_W, NA_W, GQA_HEADS * LANE, GQA_KV_HEADS * LANE, GQA_KV_HEADS * LANE,
              MLA_HEADS * LANE, MLA_HEADS * LANE, MLA_HEADS * LANE]
    return pl.pallas_call(
        _qkv_kernel,
        grid=(T // tm,),
        in_specs=[rows(D), whole(wqkv), pos(LANE), pos(LANE), pos(LANE), whole(gq), whole(gk),
                  whole(nq), whole(nkv), whole(wuq), whole(wuk), whole(wuv)],
        out_specs=[rows(w) for w in widths],
        out_shape=[jax.ShapeDtypeStruct((T, w), BF16) for w in widths],
        compiler_params=_cparams(("parallel",)),
        name="qkv_proj",
    )(x2, wqkv, cs, mq, mk, gq, gk, nq, nkv, wuq, wuk, wuv)


NA_ROWS_PER_STEP = 8


def _na_kernel(q_ref, k_ref, v_ref, bias_ref, o_ref, *, n_rows):
    j = pl.program_id(1)
    win = NA_WIN_H * GRID_W
    lane = lax.broadcasted_iota(jnp.int32, (GRID_W, LANE), 1)
    low = lane < HEAD_DIM

    def row_body(a, carry):
        r = j * NA_ROWS_PER_STEP + a
        r0 = jnp.clip(r - NA_WIN_H // 2, 0, n_rows - NA_WIN_H)
        variant = r - r0
        qrow = pl.ds(pl.multiple_of(a * GRID_W, GRID_W), GRID_W)
        krow = pl.ds(pl.multiple_of(r0 * GRID_W, GRID_W), win)
        for pair in range(NA_HEADS // 2):
            cols = slice(pair * LANE, (pair + 1) * LANE)
            qp = q_ref[qrow, cols]
            kp = k_ref[krow, cols]
            vp = v_ref[krow, cols]
            outs = []
            for half in range(2):
                qm = jnp.where(low if half == 0 else jnp.logical_not(low), qp, jnp.zeros_like(qp))
                s = _dot_nt(qm, kp) + bias_ref[variant, 2 * pair + half]
                m = jnp.max(s, axis=-1, keepdims=True)
                p = jnp.exp(s - m)
                l = jnp.sum(p, axis=-1, keepdims=True)
                outs.append(_dot(p.astype(BF16), vp) / l)
            o_ref[qrow, cols] = jnp.where(low, outs[0], outs[1]).astype(BF16)
        return carry

    lax.fori_loop(0, NA_ROWS_PER_STEP, row_body, 0)


def _na_call(q, k, v, bias, *, batch, seq):
    n_rows = seq // GRID_W
    steps = n_rows // NA_ROWS_PER_STEP
    tq = NA_ROWS_PER_STEP * GRID_W
    return pl.pallas_call(
        functools.partial(_na_kernel, n_rows=n_rows),
        grid=(batch, steps),
        in_specs=[pl.BlockSpec((tq, NA_W), lambda b, j: (b * steps + j, 0)),
                  pl.BlockSpec((seq, NA_W), lambda b, j: (b, 0)),
                  pl.BlockSpec((seq, NA_W), lambda b, j: (b, 0)),
                  pl.BlockSpec(bias.shape, lambda b, j: (0, 0, 0, 0))],
        out_specs=pl.BlockSpec((tq, NA_W), lambda b, j: (b * steps + j, 0)),
        out_shape=jax.ShapeDtypeStruct(q.shape, BF16),
        compiler_params=_cparams(("parallel", "arbitrary")),
        name="na_attn",
    )(q, k, v, bias)


def _na_bias_table(rpb):
    cols = jnp.arange(GRID_W)
    c0 = jnp.clip(cols - NA_WIN_W // 2, 0, GRID_W - NA_WIN_W)
    in_win = (cols[None, :] >= c0[:, None]) & (cols[None, :] < c0[:, None] + NA_WIN_W)
    idx_c = jnp.clip(cols[None, :] - cols[:, None] + (NA_WIN_W - 1), 0, 2 * NA_WIN_W - 2)
    variant = jnp.arange(NA_WIN_H)
    idx_r = jnp.arange(NA_WIN_H)[None, :] - variant[:, None] + (NA_WIN_H - 1)
    b = rpb.astype(F32)[:, idx_r]
    b = b[..., idx_c]
    b = jnp.where(in_win[None, None, None], b, NEG_INF)
    b = b.transpose(1, 0, 3, 2, 4)
    return b.reshape(NA_WIN_H, NA_HEADS, GRID_W, NA_WIN_H * GRID_W)


def _flash_kernel(q_ref, k_ref, v_ref, o_ref, s0, s1, p0, p1, a0, a1, m_scr, acc_scr,
                  *, units, tu, tk, nk, sum_lane):
    s_slot, p_slot, a_slot = (s0, s1), (p0, p1), (a0, a1)
    n_units = len(units)
    m_scr[...] = jnp.full(m_scr.shape, NEG_INF, F32)
    acc_scr[...] = jnp.zeros(acc_scr.shape, F32)

    def chunk(ref, c):
        start = c * tk if isinstance(c, int) else pl.multiple_of(c * tk, tk)
        return ref[pl.ds(start, tk), :]

    def scores(c, slot):
        k = chunk(k_ref, c)
        for u, (r0, c0) in enumerate(units):
            s_slot[slot][u] = _dot_nt(q_ref[r0:r0 + tu, c0:c0 + LANE], k)

    def softmax(slot):
        for u in range(n_units):
            s = s_slot[slot][u]
            m_prev = m_scr[u]
            m_new = jnp.maximum(m_prev, jnp.max(s, axis=-1, keepdims=True))
            a_slot[slot][u] = jnp.exp2(m_prev - m_new)
            p_slot[slot][u] = jnp.exp2(s - m_new[:, :1]).astype(BF16)
            m_scr[u] = m_new

    def accumulate(c, slot):
        v = chunk(v_ref, c)
        for u in range(n_units):
            acc_scr[u] = a_slot[slot][u] * acc_scr[u] + _dot(p_slot[slot][u], v)

    scores(0, 0)
    scores(1, 1)
    softmax(0)

    def body(j, carry):
        c1 = 2 * j + 1
        scores(c1 + 1, 0)
        accumulate(c1 - 1, 0)
        softmax(1)
        scores(c1 + 2, 1)
        accumulate(c1, 1)
        softmax(0)
        return carry

    lax.fori_loop(0, (nk - 2) // 2, body, 0)
    accumulate(nk - 2, 0)
    softmax(1)
    accumulate(nk - 1, 1)
    for u, (r0, c0) in enumerate(units):
        acc = acc_scr[u]
        o_ref[r0:r0 + tu, c0:c0 + LANE] = (acc / acc[:, sum_lane:sum_lane + 1]).astype(BF16)


def _flash_call(q, k, v, *, batch, seq, kv_heads, units, tu, tk, sum_lane, name):
    rows = max(r0 for r0, _ in units) + tu
    width = max(c0 for _, c0 in units) + LANE
    nq = seq // rows
    nk = seq // tk
    assert nk >= 2 and nk % 2 == 0
    n_units = len(units)
    return pl.pallas_call(
        functools.partial(_flash_kernel, units=units, tu=tu, tk=tk, nk=nk, sum_lane=sum_lane),
        grid=(batch, kv_heads, nq),
        in_specs=[pl.BlockSpec((rows, width), lambda b, g, i: (b * nq + i, g)),
                  pl.BlockSpec((seq, LANE), lambda b, g, i: (b, g)),
                  pl.BlockSpec((seq, LANE), lambda b, g, i: (b, g))],
        out_specs=pl.BlockSpec((rows, width), lambda b, g, i: (b * nq + i, g)),
        out_shape=jax.ShapeDtypeStruct(q.shape, BF16),
        scratch_shapes=[pltpu.VMEM((n_units, tu, tk), F32), pltpu.VMEM((n_units, tu, tk), F32),
                        pltpu.VMEM((n_units, tu, tk), BF16), pltpu.VMEM((n_units, tu, tk), BF16),
                        pltpu.VMEM((n_units, tu, LANE), F32), pltpu.VMEM((n_units, tu, LANE), F32),
                        pltpu.VMEM((n_units, tu, LANE), F32), pltpu.VMEM((n_units, tu, LANE), F32)],
        compiler_params=_cparams(("parallel", "parallel", "arbitrary")),
        name=name,
    )(q, k, v)


def _layer_norm(z, g, b):
    mu = jnp.mean(z, axis=-1, keepdims=True)
    zc = z - mu
    var = jnp.mean(zc * zc, axis=-1, keepdims=True)
    return zc * lax.rsqrt(var + LN_EPS) * g + b


def _merge_kernel(x_ref, oa_ref, ob_ref, oc_ref, wg_ref, wa_ref, wb_ref, wc_ref, wo_ref,
                  lng_ref, lnb_ref, wrh_ref, wrl_ref, br_ref, x1_ref, x1b_ref, logit_ref, *, alpha):
    d = x_ref.shape[1]
    x = x_ref[...]
    xb = x.astype(BF16)
    mixed = None
    for i, (o_ref, w_ref) in enumerate(((oa_ref, wa_ref), (ob_ref, wb_ref), (oc_ref, wc_ref))):
        gate = jax.nn.sigmoid(_dot(xb, wg_ref[:, i * d:(i + 1) * d]))
        term = gate * _dot(o_ref[...], w_ref[...])
        mixed = term if mixed is None else mixed + term
    z = alpha * x + _dot(mixed.astype(BF16), wo_ref[...])
    x1 = _layer_norm(z, lng_ref[...], lnb_ref[...])
    x1_ref[...] = x1
    hi = x1.astype(BF16)
    lo = (x1 - hi.astype(F32)).astype(BF16)
    x1b_ref[...] = hi
    logit_ref[...] = (_dot(hi, wrh_ref[...]) + _dot(lo, wrh_ref[...]) + _dot(hi, wrl_ref[...])
                      + br_ref[...])


def _merge_call(x2, oa, ob, oc, wg, wa, wb, wc, wo, lng, lnb, wrh, wrl, br, *, alpha, tm):
    T, D = x2.shape

    def rows(width):
        return pl.BlockSpec((tm, width), lambda i: (i, 0))

    def whole(a):
        return pl.BlockSpec(a.shape, lambda i: (0,) * a.ndim)

    return pl.pallas_call(
        functools.partial(_merge_kernel, alpha=alpha),
        grid=(T // tm,),
        in_specs=[rows(D), rows(oa.shape[1]), rows(ob.shape[1]), rows(oc.shape[1]),
                  whole(wg), whole(wa), whole(wb), whole(wc), whole(wo),
                  whole(lng), whole(lnb), whole(wrh), whole(wrl), whole(br)],
        out_specs=[rows(D), rows(D), rows(LANE)],
        out_shape=[jax.ShapeDtypeStruct((T, D), F32), jax.ShapeDtypeStruct((T, D), BF16),
                   jax.ShapeDtypeStruct((T, LANE), F32)],
        compiler_params=_cparams(("parallel",)),
        name="merge_ln_router",
    )(x2, oa, ob, oc, wg, wa, wb, wc, wo, lng, lnb, wrh, wrl, br)


def _moe_kernel(be_ref, na_ref, xs_ref, wgu_ref, bgu_ref, wd_ref, bd_ref, y_ref, wgu_bf, wd_bf):
    i = pl.program_id(0)

    @pl.when(i < na_ref[0])
    def _():
        @pl.when(jnp.logical_or(i == 0, be_ref[i] != be_ref[jnp.maximum(i - 1, 0)]))
        def _():
            wgu_bf[...] = wgu_ref[0].astype(BF16)
            wd_bf[...] = wd_ref[0].astype(BF16)

        h = _dot(xs_ref[...], wgu_bf[...]) + bgu_ref[0]
        g = jnp.minimum(h[:, :D_EXPERT], SWIGLU_LIMIT)
        u = jnp.clip(h[:, D_EXPERT:], -SWIGLU_LIMIT, SWIGLU_LIMIT)
        a = g * jax.nn.sigmoid(SWIGLU_ALPHA * g) * (u + 1.0)
        y_ref[...] = (_dot(a.astype(BF16), wd_bf[...]) + bd_ref[0]).astype(y_ref.dtype)


def _moe_call(block_expert, n_active, xs, wgu, bgu, wd, bd):
    P, D = xs.shape
    n_blocks = P // MOE_BLOCK

    def blk(i, be, na):
        return (jnp.minimum(i, na[0] - 1), 0)

    def per_expert(i, be, na):
        return (be[jnp.minimum(i, na[0] - 1)], 0, 0)

    grid_spec = pltpu.PrefetchScalarGridSpec(
        num_scalar_prefetch=2,
        grid=(n_blocks,),
        in_specs=[pl.BlockSpec((MOE_BLOCK, D), blk),
                  pl.BlockSpec((1, D, 2 * D_EXPERT), per_expert),
                  pl.BlockSpec((1, 1, 2 * D_EXPERT), per_expert),
                  pl.BlockSpec((1, D_EXPERT, D), per_expert),
                  pl.BlockSpec((1, 1, D), per_expert)],
        out_specs=pl.BlockSpec((MOE_BLOCK, D), blk),
        scratch_shapes=[pltpu.VMEM((D, 2 * D_EXPERT), BF16), pltpu.VMEM((D_EXPERT, D), BF16)],
    )
    return pl.pallas_call(
        _moe_kernel,
        grid_spec=grid_spec,
        out_shape=jax.ShapeDtypeStruct((P, D), BF16),
        compiler_params=_cparams(("arbitrary",)),
        name="moe_ffn",
    )(block_expert, n_active, xs, wgu, bgu, wd, bd)


def _final_kernel(x_ref, y_ref, gate_ref, lng_ref, lnb_ref, o_ref, *, alpha):
    gate = gate_ref[...]
    f = None
    for k in range(TOP_K):
        term = y_ref[k].astype(F32) * gate[:, k:k + 1]
        f = term if f is None else f + term
    o_ref[...] = _layer_norm(alpha * x_ref[...] + f, lng_ref[...], lnb_ref[...])


def _final_call(x1, yk, gate, lng, lnb, *, alpha, tm):
    T, D = x1.shape
    return pl.pallas_call(
        functools.partial(_final_kernel, alpha=alpha),
        grid=(T // tm,),
        in_specs=[pl.BlockSpec((tm, D), lambda i: (i, 0)),
                  pl.BlockSpec((TOP_K, tm, D), lambda i: (0, i, 0)),
                  pl.BlockSpec((tm, TOP_K), lambda i: (i, 0)),
                  pl.BlockSpec((1, D), lambda i: (0, 0)),
                  pl.BlockSpec((1, D), lambda i: (0, 0))],
        out_specs=pl.BlockSpec((tm, D), lambda i: (i, 0)),
        out_shape=jax.ShapeDtypeStruct((T, D), F32),
        compiler_params=_cparams(("parallel",)),
        name="combine_ln",
    )(x1, yk, gate, lng, lnb)


def _rot_half(w):
    half = w.shape[-1] // 2
    return jnp.concatenate([w[..., half:], w[..., :half]], axis=-1)


def _axial_tables(seq):
    def cos_sin(dim):
        quarter = dim // 4
        inv = ROPE_THETA ** (-jnp.arange(quarter, dtype=F32) / quarter)
        t = jnp.arange(seq)
        row = (t // GRID_W).astype(F32)
        col = (t % GRID_W).astype(F32)
        ang = jnp.concatenate([row[:, None] * inv, col[:, None] * inv], -1)
        return jnp.cos(ang), jnp.sin(ang)

    c64, s64 = cos_sin(HEAD_DIM)
    c32, s32 = cos_sin(MLA_ROPE)
    cs = jnp.concatenate([c64, c64, -s64, s64], -1)
    m = jnp.concatenate([jnp.ones((seq, MLA_NOPE), F32), c32, c32, -s32, s32], -1)
    return cs, m * (MLA_QK ** -0.5 * LOG2_E), m


def _layer_weights(w_in, gqa_q_norm, gqa_k_norm, w_uq, w_ukv, w_branch_b, w_branch_c):
    D = w_in.shape[0]
    widths = [NA_W, NA_W, NA_W, GQA_HEADS * HEAD_DIM, GQA_KV_HEADS * HEAD_DIM, GQA_KV_HEADS * HEAD_DIM,
              MLA_Q_RANK, MLA_KV_RANK, MLA_ROPE, N_BRANCH * D]
    offs = [0]
    for w in widths:
        offs.append(offs[-1] + w)
    na_q, na_k, na_v, g_q, g_k, g_v, c_q, c_kv, k_r, gates = [w_in[:, offs[i]:offs[i + 1]] for i in range(10)]

    def heads_with_rot(w, n):
        w = w.reshape(D, n, HEAD_DIM)
        return jnp.concatenate([w, _rot_half(w)], -1).reshape(D, n * LANE)

    def heads_padded(w, n, width):
        w = w.reshape(w.shape[0], n, width)
        return jnp.pad(w, ((0, 0), (0, 0), (0, LANE - width))).reshape(w.shape[0], n * LANE)

    kr_cols = jnp.concatenate([jnp.zeros((D, MLA_NOPE), F32), k_r, _rot_half(k_r)], -1)
    wqkv = jnp.concatenate([na_q * (HEAD_DIM ** -0.5), na_k, na_v,
                            heads_with_rot(g_q, GQA_HEADS), heads_with_rot(g_k, GQA_KV_HEADS),
                            heads_padded(g_v, GQA_KV_HEADS, HEAD_DIM), c_q, c_kv, kr_cols], -1).astype(BF16)

    gq = (jnp.concatenate([gqa_q_norm, _rot_half(gqa_q_norm)]) * (HEAD_DIM ** -0.5 * LOG2_E)).reshape(1, LANE)
    gk = jnp.concatenate([gqa_k_norm, _rot_half(gqa_k_norm)]).reshape(1, LANE)

    uq = w_uq.reshape(MLA_Q_RANK, MLA_HEADS, MLA_QK)
    uq_rope = uq[..., MLA_NOPE:]
    wuq = jnp.concatenate([uq, _rot_half(uq_rope)], -1).reshape(MLA_Q_RANK, MLA_HEADS * LANE).astype(BF16)
    ukv = w_ukv.reshape(MLA_KV_RANK, MLA_HEADS, MLA_NOPE + MLA_V)
    wuk = heads_padded(ukv[..., :MLA_NOPE].reshape(MLA_KV_RANK, -1), MLA_HEADS, MLA_NOPE).astype(BF16)
    wuv = heads_padded(ukv[..., MLA_NOPE:].reshape(MLA_KV_RANK, -1), MLA_HEADS, MLA_V).astype(BF16)

    def rows_padded(w, n, width):
        w = w.reshape(n, width, D)
        return jnp.pad(w, ((0, 0), (0, LANE - width), (0, 0))).reshape(n * LANE, D).astype(BF16)

    wb = rows_padded(w_branch_b, GQA_HEADS, HEAD_DIM)
    wc = rows_padded(w_branch_c, MLA_HEADS, MLA_V)
    return wqkv, gq, gk, wuq, wuk, wuv, gates.astype(BF16), wb, wc


def _routing(logits, n_tokens):
    top_val, top_idx = lax.top_k(logits, TOP_K)
    gate = jax.nn.softmax(top_val, axis=-1)
    sel = jnp.sum(jax.nn.one_hot(top_idx, N_EXPERTS, dtype=jnp.int32), axis=1)
    incl = jnp.cumsum(sel, axis=0)
    counts = incl[-1]
    rank = incl - sel
    padded = ((counts + MOE_BLOCK - 1) // MOE_BLOCK) * MOE_BLOCK
    cum_padded = jnp.cumsum(padded)
    starts = cum_padded - padded
    dest = jnp.take_along_axis(starts[None, :] + rank, top_idx, axis=1).astype(jnp.int32)
    n_blocks = n_tokens * TOP_K // MOE_BLOCK + N_EXPERTS
    slots = n_blocks * MOE_BLOCK
    tok = jnp.broadcast_to(jnp.arange(n_tokens, dtype=jnp.int32)[:, None], dest.shape)
    slot_token = jnp.zeros((slots,), jnp.int32).at[dest.reshape(-1)].set(tok.reshape(-1))
    block_start = jnp.arange(n_blocks, dtype=jnp.int32) * MOE_BLOCK
    block_expert = jnp.minimum(jnp.searchsorted(cum_padded, block_start, side='right'),
                               N_EXPERTS - 1).astype(jnp.int32)
    n_active = (cum_padded[-1] // MOE_BLOCK).astype(jnp.int32).reshape(1)
    return gate, dest, slot_token, block_expert, n_active


def kernel(x, w_in, na_rpb, gqa_q_norm, gqa_k_norm, mla_q_norm, mla_kv_norm, w_uq, w_ukv, w_branch_a, w_branch_b, w_branch_c, w_out, ln1_g, ln1_b, w_router, b_router, w_gate_up, b_gate_up, w_down, b_down, ln2_g, ln2_b):
    B, S, D = x.shape
    T = B * S
    depth = w_in.shape[0]
    alpha = (2.0 * depth) ** 0.25
    tm = min(512, S)
    tq = min(512, S)
    cs, mq_tab, mk_tab = _axial_tables(S)
    x2 = x.reshape(T, D)
    for l in range(depth):
        wqkv, gq, gk, wuq, wuk, wuv, wg, wb, wc = _layer_weights(
            w_in[l], gqa_q_norm[l], gqa_k_norm[l], w_uq[l], w_ukv[l], w_branch_b[l], w_branch_c[l])
        naq, nak, nav, gqo, gko, gvo, mqo, mko, mvo = _qkv_call(
            x2, wqkv, cs, mq_tab, mk_tab, gq, gk, mla_q_norm[l].reshape(1, -1), mla_kv_norm[l].reshape(1, -1),
            wuq, wuk, wuv, seq=S, tm=tm)
        oa = _na_call(naq, nak, nav, _na_bias_table(na_rpb[l]), batch=B, seq=S)
        ob = _flash_call(gqo, gko, gvo, batch=B, seq=S, kv_heads=GQA_KV_HEADS,
                         units=tuple((0, r * LANE) for r in range(GQA_REP)),
                         tu=tq, tk=tq, sum_lane=HEAD_DIM, name="gqa_attn")
        oc = _flash_call(mqo, mko, mvo, batch=B, seq=S, kv_heads=MLA_HEADS,
                         units=((0, 0), (tq, 0)), tu=tq, tk=tq, sum_lane=MLA_V, name="mla_attn")
        wr = jnp.pad(w_router[l], ((0, 0), (0, LANE - N_EXPERTS)))
        wrh = wr.astype(BF16)
        wrl = (wr - wrh.astype(F32)).astype(BF16)
        br = jnp.pad(b_router[l], (0, LANE - N_EXPERTS)).reshape(1, LANE)
        x1, x1b, logits = _merge_call(
            x2, oa, ob, oc, wg, w_branch_a[l].astype(BF16), wb, wc, w_out[l].astype(BF16),
            ln1_g[l].reshape(1, D), ln1_b[l].reshape(1, D), wrh, wrl, br, alpha=alpha, tm=tm)
        gate, dest, slot_token, block_expert, n_active = _routing(logits[:, :N_EXPERTS], T)
        xs = jnp.take(x1b, slot_token, axis=0)
        y = _moe_call(block_expert, n_active, xs, w_gate_up[l],
                      b_gate_up[l].reshape(N_EXPERTS, 1, -1), w_down[l],
                      b_down[l].reshape(N_EXPERTS, 1, -1))
        yk = jnp.take(y, dest.T.reshape(-1), axis=0).reshape(TOP_K, T, D)
        x2 = _final_call(x1, yk, gate, ln2_g[l].reshape(1, D), ln2_b[l].reshape(1, D), alpha=alpha, tm=tm)
    return x2.reshape(B, S, D)
```

```python
import functools
import math

import jax
import jax.numpy as jnp
from jax import lax
from jax.experimental import pallas as pl
from jax.experimental.pallas import tpu as pltpu

F32 = jnp.float32
BF16 = jnp.bfloat16

LANE = 128
GRID_W = 64
HEAD_DIM = 64
NA_HEADS = 6
NA_WIN_H = 8
NA_WIN_W = 16
NA_W = NA_HEADS * HEAD_DIM
GQA_HEADS = 6
GQA_KV_HEADS = 2
GQA_REP = GQA_HEADS // GQA_KV_HEADS
MLA_HEADS = 4
MLA_Q_RANK = 384
MLA_KV_RANK = 256
MLA_NOPE = 64
MLA_ROPE = 32
MLA_V = 64
MLA_QK = MLA_NOPE + MLA_ROPE
ROPE_THETA = 10000.0
N_BRANCH = 3
N_EXPERTS = 32
TOP_K = 4
D_EXPERT = 1024
SWIGLU_LIMIT = 7.0
SWIGLU_ALPHA = 1.702
MOE_BLOCK = 512
LN_EPS = 1e-5
RMS_EPS = 1e-6
NEG_INF = -1e30
LOG2_E = math.log2(math.e)
VMEM_LIMIT = 56 * 1024 * 1024

_C_NAQ = 0
_C_NAK = _C_NAQ + NA_W
_C_NAV = _C_NAK + NA_W
_C_GQ = _C_NAV + NA_W
_C_GK = _C_GQ + GQA_HEADS * LANE
_C_GV = _C_GK + GQA_KV_HEADS * LANE
_C_CQ = _C_GV + GQA_KV_HEADS * LANE
_C_CKV = _C_CQ + MLA_Q_RANK
_C_KR = _C_CKV + MLA_KV_RANK
_C_END = _C_KR + LANE


def _cparams(sem):
    return pltpu.CompilerParams(dimension_semantics=sem, vmem_limit_bytes=VMEM_LIMIT)


def _dot(a, b):
    return jnp.dot(a, b, preferred_element_type=F32)


def _dot_nt(a, b):
    return lax.dot_general(a, b, (((1,), (1,)), ((), ())), preferred_element_type=F32)


def _qkv_kernel(x_ref, w_ref, cs_ref, mq_ref, mk_ref, gq_ref, gk_ref, nq_ref, nkv_ref,
                wuq_ref, wuk_ref, wuv_ref,
                naq_ref, nak_ref, nav_ref, gqo_ref, gko_ref, gvo_ref, mqo_ref, mko_ref, mvo_ref):
    tm = x_ref.shape[0]
    xb = x_ref[...].astype(BF16)

    def proj(c0, width):
        return _dot(xb, w_ref[:, c0:c0 + width])

    naq_ref[...] = proj(_C_NAQ, NA_W).astype(BF16)
    nak_ref[...] = proj(_C_NAK, NA_W).astype(BF16)
    nav_ref[...] = proj(_C_NAV, NA_W).astype(BF16)

    lane = lax.broadcasted_iota(jnp.int32, (tm, LANE), 1)
    cs = cs_ref[...]

    def norm_rope(hc, gain):
        r = lax.rsqrt(jnp.mean(hc * hc, axis=-1, keepdims=True) + RMS_EPS)
        a = hc * r * (gain * cs)
        return jnp.where(lane < HEAD_DIM, a + pltpu.roll(a, HEAD_DIM, 1), 0.0)

    gq_gain = gq_ref[...]
    for h in range(GQA_HEADS):
        hc = proj(_C_GQ + h * LANE, LANE)
        gqo_ref[:, h * LANE:(h + 1) * LANE] = norm_rope(hc, gq_gain).astype(BF16)
    gk_gain = gk_ref[...]
    for g in range(GQA_KV_HEADS):
        hc = proj(_C_GK + g * LANE, LANE)
        gko_ref[:, g * LANE:(g + 1) * LANE] = norm_rope(hc, gk_gain).astype(BF16)
        hv = proj(_C_GV + g * LANE, LANE)
        gvo_ref[:, g * LANE:(g + 1) * LANE] = jnp.where(lane == HEAD_DIM, 1.0, hv).astype(BF16)

    def rms(v, gain):
        r = lax.rsqrt(jnp.mean(v * v, axis=-1, keepdims=True) + RMS_EPS)
        return (v * r * gain).astype(BF16)

    def mla_rope(b):
        summed = b + pltpu.roll(b, LANE - MLA_ROPE, 1)
        return jnp.where(lane < MLA_NOPE, b, jnp.where(lane < MLA_QK, summed, 0.0))

    hq = _dot(rms(proj(_C_CQ, MLA_Q_RANK), nq_ref[...]), wuq_ref[...])
    mq = mq_ref[...]
    for h in range(MLA_HEADS):
        mqo_ref[:, h * LANE:(h + 1) * LANE] = mla_rope(hq[:, h * LANE:(h + 1) * LANE] * mq).astype(BF16)

    ckv = rms(proj(_C_CKV, MLA_KV_RANK), nkv_ref[...])
    hk = _dot(ckv, wuk_ref[...])
    hv = _dot(ckv, wuv_ref[...])
    kr = mla_rope(proj(_C_KR, LANE) * mk_ref[...])
    for h in range(MLA_HEADS):
        sl = slice(h * LANE, (h + 1) * LANE)
        mko_ref[:, sl] = (hk[:, sl] + kr).astype(BF16)
        mvo_ref[:, sl] = jnp.where(lane == MLA_V, 1.0, hv[:, sl]).astype(BF16)


def _qkv_call(x2, wqkv, cs, mq, mk, gq, gk, nq, nkv, wuq, wuk, wuv, *, seq, tm):
    T, D = x2.shape
    ns = seq // tm

    def rows(width):
        return pl.BlockSpec((tm, width), lambda i: (i, 0))

    def pos(width):
        return pl.BlockSpec((tm, width), lambda i: (i % ns, 0))

    def whole(a):
        return pl.BlockSpec(a.shape, lambda i: (0,) * a.ndim)

    widths = [NA_W, NA_W, NA_W, GQA_HEADS * LANE, GQA_KV_HEADS * LANE, GQA_KV_HEADS * LANE,
              MLA_HEADS * LANE, MLA_HEADS * LANE, MLA_HEADS * LANE]
    return pl.pallas_call(
        _qkv_kernel,
        grid=(T // tm,),
        in_specs=[rows(D), whole(wqkv), pos(LANE), pos(LANE), pos(LANE), whole(gq), whole(gk),
                  whole(nq), whole(nkv), whole(wuq), whole(wuk), whole(wuv)],
        out_specs=[rows(w) for w in widths],
        out_shape=[jax.ShapeDtypeStruct((T, w), BF16) for w in widths],
        compiler_params=_cparams(("parallel",)),
        name="qkv_proj",
    )(x2, wqkv, cs, mq, mk, gq, gk, nq, nkv, wuq, wuk, wuv)


NA_ROWS_PER_STEP = 8
NA_ROWS_PER_ITER = 2


def _na_kernel(q_ref, k_ref, v_ref, bias_ref, o_ref, *, n_rows):
    j = pl.program_id(1)
    win = NA_WIN_H * GRID_W
    lane = lax.broadcasted_iota(jnp.int32, (GRID_W, LANE), 1)
    low = lane < HEAD_DIM

    def rows_body(a2, carry):
        work = []
        for rr in range(NA_ROWS_PER_ITER):
            a = a2 * NA_ROWS_PER_ITER + rr
            r = j * NA_ROWS_PER_STEP + a
            r0 = jnp.clip(r - NA_WIN_H // 2, 0, n_rows - NA_WIN_H)
            variant = r - r0
            qrow = pl.ds(pl.multiple_of(a * GRID_W, GRID_W), GRID_W)
            krow = pl.ds(pl.multiple_of(r0 * GRID_W, GRID_W), win)
            for pair in range(NA_HEADS // 2):
                cols = slice(pair * LANE, (pair + 1) * LANE)
                qp = q_ref[qrow, cols]
                kp = k_ref[krow, cols]
                for half in range(2):
                    qm = jnp.where(low if half == 0 else jnp.logical_not(low), qp, jnp.zeros_like(qp))
                    s = _dot_nt(qm, kp) + bias_ref[variant, 2 * pair + half]
                    work.append((qrow, krow, cols, half, s))
        probs = []
        for qrow, krow, cols, half, s in work:
            m = jnp.max(s, axis=-1, keepdims=True)
            p = jnp.exp(s - m)
            probs.append((p.astype(BF16), jnp.sum(p, axis=-1, keepdims=True)))
        outs = []
        for (qrow, krow, cols, half, _), (p, l) in zip(work, probs):
            outs.append(_dot(p, v_ref[krow, cols]) / l)
        for n in range(0, len(work), 2):
            qrow, _, cols, _, _ = work[n]
            o_ref[qrow, cols] = jnp.where(low, outs[n], outs[n + 1]).astype(BF16)
        return carry

    lax.fori_loop(0, NA_ROWS_PER_STEP // NA_ROWS_PER_ITER, rows_body, 0)


def _na_call(q, k, v, bias, *, batch, seq):
    n_rows = seq // GRID_W
    steps = n_rows // NA_ROWS_PER_STEP
    tq = NA_ROWS_PER_STEP * GRID_W
    return pl.pallas_call(
        functools.partial(_na_kernel, n_rows=n_rows),
        grid=(batch, steps),
        in_specs=[pl.BlockSpec((tq, NA_W), lambda b, j: (b * steps + j, 0)),
                  pl.BlockSpec((seq, NA_W), lambda b, j: (b, 0)),
                  pl.BlockSpec((seq, NA_W), lambda b, j: (b, 0)),
                  pl.BlockSpec(bias.shape, lambda b, j: (0, 0, 0, 0))],
        out_specs=pl.BlockSpec((tq, NA_W), lambda b, j: (b * steps + j, 0)),
        out_shape=jax.ShapeDtypeStruct(q.shape, BF16),
        compiler_params=_cparams(("parallel", "arbitrary")),
        name="na_attn",
    )(q, k, v, bias)


def _na_bias_table(rpb):
    cols = jnp.arange(GRID_W)
    c0 = jnp.clip(cols - NA_WIN_W // 2, 0, GRID_W - NA_WIN_W)
    in_win = (cols[None, :] >= c0[:, None]) & (cols[None, :] < c0[:, None] + NA_WIN_W)
    idx_c = jnp.clip(cols[None, :] - cols[:, None] + (NA_WIN_W - 1), 0, 2 * NA_WIN_W - 2)
    variant = jnp.arange(NA_WIN_H)
    idx_r = jnp.arange(NA_WIN_H)[None, :] - variant[:, None] + (NA_WIN_H - 1)
    b = rpb.astype(F32)[:, idx_r]
    b = b[..., idx_c]
    b = jnp.where(in_win[None, None, None], b, NEG_INF)
    b = b.transpose(1, 0, 3, 2, 4)
    return b.reshape(NA_WIN_H, NA_HEADS, GRID_W, NA_WIN_H * GRID_W)


def _flash_kernel(q_ref, k_ref, v_ref, o_ref, s0, s1, p0, p1, a0, a1, m_scr, acc_scr,
                  *, units, tu, tk, nk, sum_lane):
    s_slot, p_slot, a_slot = (s0, s1), (p0, p1), (a0, a1)
    n_units = len(units)
    m_scr[...] = jnp.full(m_scr.shape, NEG_INF, F32)
    acc_scr[...] = jnp.zeros(acc_scr.shape, F32)

    def chunk(ref, c):
        start = c * tk if isinstance(c, int) else pl.multiple_of(c * tk, tk)
        return ref[pl.ds(start, tk), :]

    def scores(c, slot):
        k = chunk(k_ref, c)
        for u, (r0, c0) in enumerate(units):
            s_slot[slot][u] = _dot_nt(q_ref[r0:r0 + tu, c0:c0 + LANE], k)

    def softmax(slot):
        for u in range(n_units):
            s = s_slot[slot][u]
            m_prev = m_scr[u]
            m_new = jnp.maximum(m_prev, jnp.max(s, axis=-1, keepdims=True))
            a_slot[slot][u] = jnp.exp2(m_prev - m_new)
            p_slot[slot][u] = jnp.exp2(s - m_new[:, :1]).astype(BF16)
            m_scr[u] = m_new

    def accumulate(c, slot):
        v = chunk(v_ref, c)
        for u in range(n_units):
            acc_scr[u] = a_slot[slot][u] * acc_scr[u] + _dot(p_slot[slot][u], v)

    scores(0, 0)
    scores(1, 1)
    softmax(0)

    def body(j, carry):
        c1 = 2 * j + 1
        scores(c1 + 1, 0)
        accumulate(c1 - 1, 0)
        softmax(1)
        scores(c1 + 2, 1)
        accumulate(c1, 1)
        softmax(0)
        return carry

    lax.fori_loop(0, (nk - 2) // 2, body, 0)
    accumulate(nk - 2, 0)
    softmax(1)
    accumulate(nk - 1, 1)
    for u, (r0, c0) in enumerate(units):
        acc = acc_scr[u]
        o_ref[r0:r0 + tu, c0:c0 + LANE] = (acc / acc[:, sum_lane:sum_lane + 1]).astype(BF16)


def _flash_call(q, k, v, *, batch, seq, kv_heads, units, tu, tk, sum_lane, name):
    rows = max(r0 for r0, _ in units) + tu
    width = max(c0 for _, c0 in units) + LANE
    nq = seq // rows
    nk = seq // tk
    assert nk >= 2 and nk % 2 == 0
    n_units = len(units)
    return pl.pallas_call(
        functools.partial(_flash_kernel, units=units, tu=tu, tk=tk, nk=nk, sum_lane=sum_lane),
        grid=(batch, kv_heads, nq),
        in_specs=[pl.BlockSpec((rows, width), lambda b, g, i: (b * nq + i, g)),
                  pl.BlockSpec((seq, LANE), lambda b, g, i: (b, g)),
                  pl.BlockSpec((seq, LANE), lambda b, g, i: (b, g))],
        out_specs=pl.BlockSpec((rows, width), lambda b, g, i: (b * nq + i, g)),
        out_shape=jax.ShapeDtypeStruct(q.shape, BF16),
        scratch_shapes=[pltpu.VMEM((n_units, tu, tk), F32), pltpu.VMEM((n_units, tu, tk), F32),
                        pltpu.VMEM((n_units, tu, tk), BF16), pltpu.VMEM((n_units, tu, tk), BF16),
                        pltpu.VMEM((n_units, tu, LANE), F32), pltpu.VMEM((n_units, tu, LANE), F32),
                        pltpu.VMEM((n_units, tu, LANE), F32), pltpu.VMEM((n_units, tu, LANE), F32)],
        compiler_params=_cparams(("parallel", "parallel", "arbitrary")),
        name=name,
    )(q, k, v)


def _layer_norm(z, g, b):
    mu = jnp.mean(z, axis=-1, keepdims=True)
    zc = z - mu
    var = jnp.mean(zc * zc, axis=-1, keepdims=True)
    return zc * lax.rsqrt(var + LN_EPS) * g + b


def _merge_kernel(x_ref, oa_ref, ob_ref, oc_ref, wg_ref, wa_ref, wb_ref, wc_ref, wo_ref,
                  lng_ref, lnb_ref, wrh_ref, wrl_ref, br_ref, x1_ref, x1b_ref, logit_ref, *, alpha):
    d = x_ref.shape[1]
    x = x_ref[...]
    xb = x.astype(BF16)
    mixed = None
    for i, (o_ref, w_ref) in enumerate(((oa_ref, wa_ref), (ob_ref, wb_ref), (oc_ref, wc_ref))):
        gate = jax.nn.sigmoid(_dot(xb, wg_ref[:, i * d:(i + 1) * d]))
        term = gate * _dot(o_ref[...], w_ref[...])
        mixed = term if mixed is None else mixed + term
    z = alpha * x + _dot(mixed.astype(BF16), wo_ref[...])
    x1 = _layer_norm(z, lng_ref[...], lnb_ref[...])
    x1_ref[...] = x1
    hi = x1.astype(BF16)
    lo = (x1 - hi.astype(F32)).astype(BF16)
    x1b_ref[...] = hi
    logit_ref[...] = (_dot(hi, wrh_ref[...]) + _dot(lo, wrh_ref[...]) + _dot(hi, wrl_ref[...])
                      + br_ref[...])


def _merge_call(x2, oa, ob, oc, wg, wa, wb, wc, wo, lng, lnb, wrh, wrl, br, *, alpha, tm):
    T, D = x2.shape

    def rows(width):
        return pl.BlockSpec((tm, width), lambda i: (i, 0))

    def whole(a):
        return pl.BlockSpec(a.shape, lambda i: (0,) * a.ndim)

    return pl.pallas_call(
        functools.partial(_merge_kernel, alpha=alpha),
        grid=(T // tm,),
        in_specs=[rows(D), rows(oa.shape[1]), rows(ob.shape[1]), rows(oc.shape[1]),
                  whole(wg), whole(wa), whole(wb), whole(wc), whole(wo),
                  whole(lng), whole(lnb), whole(wrh), whole(wrl), whole(br)],
        out_specs=[rows(D), rows(D), rows(LANE)],
        out_shape=[jax.ShapeDtypeStruct((T, D), F32), jax.ShapeDtypeStruct((T, D), BF16),
                   jax.ShapeDtypeStruct((T, LANE), F32)],
        compiler_params=_cparams(("parallel",)),
        name="merge_ln_router",
    )(x2, oa, ob, oc, wg, wa, wb, wc, wo, lng, lnb, wrh, wrl, br)


def _moe_kernel(be_ref, na_ref, xs_ref, wgu_ref, bgu_ref, wd_ref, bd_ref, y_ref, wgu_bf, wd_bf):
    i = pl.program_id(0)

    @pl.when(i < na_ref[0])
    def _():
        @pl.when(jnp.logical_or(i == 0, be_ref[i] != be_ref[jnp.maximum(i - 1, 0)]))
        def _():
            wgu_bf[...] = wgu_ref[0, 0].astype(BF16)
            wd_bf[...] = wd_ref[0, 0].astype(BF16)

        h = _dot(xs_ref[...], wgu_bf[...]) + bgu_ref[0, 0]
        g = jnp.minimum(h[:, :D_EXPERT], SWIGLU_LIMIT)
        u = jnp.clip(h[:, D_EXPERT:], -SWIGLU_LIMIT, SWIGLU_LIMIT)
        a = g * jax.nn.sigmoid(SWIGLU_ALPHA * g) * (u + 1.0)
        y_ref[...] = (_dot(a.astype(BF16), wd_bf[...]) + bd_ref[0, 0]).astype(y_ref.dtype)


def _moe_call(block_expert, n_active, xs, wgu, bgu, wd, bd, *, layer):
    P, D = xs.shape
    n_blocks = P // MOE_BLOCK

    def blk(i, be, na):
        return (jnp.minimum(i, na[0] - 1), 0)

    def per_expert(i, be, na):
        return (layer, be[jnp.minimum(i, na[0] - 1)], 0, 0)

    grid_spec = pltpu.PrefetchScalarGridSpec(
        num_scalar_prefetch=2,
        grid=(n_blocks,),
        in_specs=[pl.BlockSpec((MOE_BLOCK, D), blk),
                  pl.BlockSpec((1, 1, D, 2 * D_EXPERT), per_expert),
                  pl.BlockSpec((1, 1, 1, 2 * D_EXPERT), per_expert),
                  pl.BlockSpec((1, 1, D_EXPERT, D), per_expert),
                  pl.BlockSpec((1, 1, 1, D), per_expert)],
        out_specs=pl.BlockSpec((MOE_BLOCK, D), blk),
        scratch_shapes=[pltpu.VMEM((D, 2 * D_EXPERT), BF16), pltpu.VMEM((D_EXPERT, D), BF16)],
    )
    return pl.pallas_call(
        _moe_kernel,
        grid_spec=grid_spec,
        out_shape=jax.ShapeDtypeStruct((P, D), BF16),
        compiler_params=_cparams(("arbitrary",)),
        name="moe_ffn",
    )(block_expert, n_active, xs, wgu, bgu, wd, bd)


def _final_kernel(x_ref, y_ref, gate_ref, lng_ref, lnb_ref, o_ref, *, alpha):
    gate = gate_ref[...]
    f = None
    for k in range(TOP_K):
        term = y_ref[k].astype(F32) * gate[:, k:k + 1]
        f = term if f is None else f + term
    o_ref[...] = _layer_norm(alpha * x_ref[...] + f, lng_ref[...], lnb_ref[...])


def _final_call(x1, yk, gate, lng, lnb, *, alpha, tm):
    T, D = x1.shape
    return pl.pallas_call(
        functools.partial(_final_kernel, alpha=alpha),
        grid=(T // tm,),
        in_specs=[pl.BlockSpec((tm, D), lambda i: (i, 0)),
                  pl.BlockSpec((TOP_K, tm, D), lambda i: (0, i, 0)),
                  pl.BlockSpec((tm, TOP_K), lambda i: (i, 0)),
                  pl.BlockSpec((1, D), lambda i: (0, 0)),
                  pl.BlockSpec((1, D), lambda i: (0, 0))],
        out_specs=pl.BlockSpec((tm, D), lambda i: (i, 0)),
        out_shape=jax.ShapeDtypeStruct((T, D), F32),
        compiler_params=_cparams(("parallel",)),
        name="combine_ln",
    )(x1, yk, gate, lng, lnb)


def _rot_half(w):
    half = w.shape[-1] // 2
    return jnp.concatenate([w[..., half:], w[..., :half]], axis=-1)


def _axial_tables(seq):
    def cos_sin(dim):
        quarter = dim // 4
        inv = ROPE_THETA ** (-jnp.arange(quarter, dtype=F32) / quarter)
        t = jnp.arange(seq)
        row = (t // GRID_W).astype(F32)
        col = (t % GRID_W).astype(F32)
        ang = jnp.concatenate([row[:, None] * inv, col[:, None] * inv], -1)
        return jnp.cos(ang), jnp.sin(ang)

    c64, s64 = cos_sin(HEAD_DIM)
    c32, s32 = cos_sin(MLA_ROPE)
    cs = jnp.concatenate([c64, c64, -s64, s64], -1)
    m = jnp.concatenate([jnp.ones((seq, MLA_NOPE), F32), c32, c32, -s32, s32], -1)
    return cs, m * (MLA_QK ** -0.5 * LOG2_E), m


def _layer_weights(w_in, gqa_q_norm, gqa_k_norm, w_uq, w_ukv, w_branch_b, w_branch_c):
    D = w_in.shape[0]
    widths = [NA_W, NA_W, NA_W, GQA_HEADS * HEAD_DIM, GQA_KV_HEADS * HEAD_DIM, GQA_KV_HEADS * HEAD_DIM,
              MLA_Q_RANK, MLA_KV_RANK, MLA_ROPE, N_BRANCH * D]
    offs = [0]
    for w in widths:
        offs.append(offs[-1] + w)
    na_q, na_k, na_v, g_q, g_k, g_v, c_q, c_kv, k_r, gates = [w_in[:, offs[i]:offs[i + 1]] for i in range(10)]

    def heads_with_rot(w, n):
        w = w.reshape(D, n, HEAD_DIM)
        return jnp.concatenate([w, _rot_half(w)], -1).reshape(D, n * LANE)

    def heads_padded(w, n, width):
        w = w.reshape(w.shape[0], n, width)
        return jnp.pad(w, ((0, 0), (0, 0), (0, LANE - width))).reshape(w.shape[0], n * LANE)

    kr_cols = jnp.concatenate([jnp.zeros((D, MLA_NOPE), F32), k_r, _rot_half(k_r)], -1)
    wqkv = jnp.concatenate([na_q * (HEAD_DIM ** -0.5), na_k, na_v,
                            heads_with_rot(g_q, GQA_HEADS), heads_with_rot(g_k, GQA_KV_HEADS),
                            heads_padded(g_v, GQA_KV_HEADS, HEAD_DIM), c_q, c_kv, kr_cols], -1).astype(BF16)

    gq = (jnp.concatenate([gqa_q_norm, _rot_half(gqa_q_norm)]) * (HEAD_DIM ** -0.5 * LOG2_E)).reshape(1, LANE)
    gk = jnp.concatenate([gqa_k_norm, _rot_half(gqa_k_norm)]).reshape(1, LANE)

    uq = w_uq.reshape(MLA_Q_RANK, MLA_HEADS, MLA_QK)
    uq_rope = uq[..., MLA_NOPE:]
    wuq = jnp.concatenate([uq, _rot_half(uq_rope)], -1).reshape(MLA_Q_RANK, MLA_HEADS * LANE).astype(BF16)
    ukv = w_ukv.reshape(MLA_KV_RANK, MLA_HEADS, MLA_NOPE + MLA_V)
    wuk = heads_padded(ukv[..., :MLA_NOPE].reshape(MLA_KV_RANK, -1), MLA_HEADS, MLA_NOPE).astype(BF16)
    wuv = heads_padded(ukv[..., MLA_NOPE:].reshape(MLA_KV_RANK, -1), MLA_HEADS, MLA_V).astype(BF16)

    def rows_padded(w, n, width):
        w = w.reshape(n, width, D)
        return jnp.pad(w, ((0, 0), (0, LANE - width), (0, 0))).reshape(n * LANE, D).astype(BF16)

    wb = rows_padded(w_branch_b, GQA_HEADS, HEAD_DIM)
    wc = rows_padded(w_branch_c, MLA_HEADS, MLA_V)
    return wqkv, gq, gk, wuq, wuk, wuv, gates.astype(BF16), wb, wc


def _route_kernel(logit_ref, tril_ref, gate_ref, dest_ref, cnt_ref, cnt_scr, run_scr):
    phase = pl.program_id(0)
    i = pl.program_id(1)
    tm = logit_ref.shape[0]
    lane = lax.broadcasted_iota(jnp.int32, (tm, LANE), 1).astype(F32)

    @pl.when(jnp.logical_and(phase == 0, i == 0))
    def _():
        cnt_scr[...] = jnp.zeros(cnt_scr.shape, F32)
        run_scr[...] = jnp.zeros(run_scr.shape, F32)

    logits = logit_ref[...]
    sel = jnp.zeros((tm, LANE), F32)
    vals, hits = [], []
    for _ in range(TOP_K):
        m = jnp.max(logits, axis=-1, keepdims=True)
        idx = jnp.min(jnp.where(logits == m, lane, float(LANE)), axis=-1, keepdims=True)
        hit = lane == idx
        sel = jnp.where(hit, 1.0, sel)
        logits = jnp.where(hit, -jnp.inf, logits)
        vals.append(m)
        hits.append(hit)
    incl = _dot(tril_ref[...], sel.astype(BF16))

    @pl.when(phase == 0)
    def _():
        cnt_scr[...] = cnt_scr[...] + incl[tm - 1:tm, :]

    @pl.when(phase == 1)
    def _():
        counts = cnt_scr[...]
        padded = jnp.floor((counts + (MOE_BLOCK - 1)) * (1.0 / MOE_BLOCK)) * MOE_BLOCK
        lane8 = lax.broadcasted_iota(jnp.int32, counts.shape, 1)
        cum = padded
        shift = 1
        while shift < LANE:
            cum = cum + jnp.where(lane8 >= shift, pltpu.roll(cum, shift, 1), 0.0)
            shift *= 2
        starts = (cum - padded) + run_scr[...]
        base = starts[0:1, :] + (incl - sel)
        es = [jnp.exp(v - vals[0]) for v in vals]
        denom = es[0] + es[1] + es[2] + es[3]
        gate_out = jnp.zeros((tm, LANE), F32)
        dest_out = jnp.zeros((tm, LANE), F32)
        for k in range(TOP_K):
            d_k = jnp.sum(jnp.where(hits[k], base, 0.0), axis=-1, keepdims=True)
            gate_out = jnp.where(lane == float(k), es[k] / denom, gate_out)
            dest_out = jnp.where(lane == float(k), d_k, dest_out)
        gate_ref[...] = gate_out
        dest_ref[...] = dest_out.astype(jnp.int32)
        run_scr[...] = run_scr[...] + incl[tm - 1:tm, :]
        cnt_ref[...] = counts[0:1, :]


def _route_call(logits, *, tm):
    T = logits.shape[0]
    nt = T // tm
    tril = (jnp.arange(tm)[:, None] >= jnp.arange(tm)[None, :]).astype(BF16)
    return pl.pallas_call(
        _route_kernel,
        grid=(2, nt),
        in_specs=[pl.BlockSpec((tm, LANE), lambda p, i: (i, 0)),
                  pl.BlockSpec((tm, tm), lambda p, i: (0, 0))],
        out_specs=[pl.BlockSpec((tm, LANE), lambda p, i: (p * i, 0)),
                   pl.BlockSpec((tm, LANE), lambda p, i: (p * i, 0)),
                   pl.BlockSpec((1, LANE), lambda p, i: (0, 0))],
        out_shape=[jax.ShapeDtypeStruct((T, LANE), F32), jax.ShapeDtypeStruct((T, LANE), jnp.int32),
                   jax.ShapeDtypeStruct((1, LANE), F32)],
        scratch_shapes=[pltpu.VMEM((8, LANE), F32), pltpu.VMEM((8, LANE), F32)],
        compiler_params=_cparams(("arbitrary", "arbitrary")),
        name="route",
    )(logits, tril)


def _routing(logits, n_tokens, *, tm):
    gate, dest, cnt = _route_call(logits, tm=tm)
    gate = gate[:, :TOP_K]
    dest = dest[:, :TOP_K]
    counts = cnt[0, :N_EXPERTS].astype(jnp.int32)
    padded = ((counts + MOE_BLOCK - 1) // MOE_BLOCK) * MOE_BLOCK
    cum_padded = jnp.cumsum(padded)
    starts_padded = cum_padded - padded
    starts_sorted = jnp.cumsum(counts) - counts
    n_blocks = n_tokens * TOP_K // MOE_BLOCK + N_EXPERTS
    block_start = jnp.arange(n_blocks, dtype=jnp.int32) * MOE_BLOCK
    block_expert = jnp.minimum(jnp.searchsorted(cum_padded, block_start, side='right'),
                               N_EXPERTS - 1).astype(jnp.int32)
    n_active = (cum_padded[-1] // MOE_BLOCK).astype(jnp.int32).reshape(1)
    tok = jnp.broadcast_to(jnp.arange(n_tokens, dtype=jnp.int32)[:, None], dest.shape)
    _, tok_sorted = lax.sort_key_val(dest.reshape(-1), tok.reshape(-1))
    pads_before = (starts_padded - starts_sorted)[block_expert]
    in_expert = block_start - starts_padded[block_expert]
    offs = jnp.arange(MOE_BLOCK, dtype=jnp.int32)[None, :]
    src = block_start[:, None] + offs - pads_before[:, None]
    valid = (in_expert[:, None] + offs) < counts[block_expert][:, None]
    slot_token = jnp.where(valid, jnp.take(tok_sorted, jnp.clip(src, 0, n_tokens * TOP_K - 1)), 0).reshape(-1)
    return gate, dest, slot_token, block_expert, n_active


def kernel(x, w_in, na_rpb, gqa_q_norm, gqa_k_norm, mla_q_norm, mla_kv_norm, w_uq, w_ukv, w_branch_a, w_branch_b, w_branch_c, w_out, ln1_g, ln1_b, w_router, b_router, w_gate_up, b_gate_up, w_down, b_down, ln2_g, ln2_b):
    B, S, D = x.shape
    T = B * S
    depth = w_in.shape[0]
    alpha = (2.0 * depth) ** 0.25
    tm = min(512, S)
    tq = min(512, S)
    tk = min(1024, S // 2)
    cs, mq_tab, mk_tab = _axial_tables(S)
    x2 = x.reshape(T, D)
    for l in range(depth):
        wqkv, gq, gk, wuq, wuk, wuv, wg, wb, wc = _layer_weights(
            w_in[l], gqa_q_norm[l], gqa_k_norm[l], w_uq[l], w_ukv[l], w_branch_b[l], w_branch_c[l])
        naq, nak, nav, gqo, gko, gvo, mqo, mko, mvo = _qkv_call(
            x2, wqkv, cs, mq_tab, mk_tab, gq, gk, mla_q_norm[l].reshape(1, -1), mla_kv_norm[l].reshape(1, -1),
            wuq, wuk, wuv, seq=S, tm=tm)
        oa = _na_call(naq, nak, nav, _na_bias_table(na_rpb[l]), batch=B, seq=S)
        ob = _flash_call(gqo, gko, gvo, batch=B, seq=S, kv_heads=GQA_KV_HEADS,
                         units=tuple((0, r * LANE) for r in range(GQA_REP)),
                         tu=tq, tk=tk, sum_lane=HEAD_DIM, name="gqa_attn")
        oc = _flash_call(mqo, mko, mvo, batch=B, seq=S, kv_heads=MLA_HEADS,
                         units=((0, 0), (tq, 0)), tu=tq, tk=tk, sum_lane=MLA_V, name="mla_attn")
        wr = jnp.pad(w_router[l], ((0, 0), (0, LANE - N_EXPERTS)))
        wrh = wr.astype(BF16)
        wrl = (wr - wrh.astype(F32)).astype(BF16)
        br = jnp.pad(b_router[l], (0, LANE - N_EXPERTS), constant_values=NEG_INF).reshape(1, LANE)
        x1, x1b, logits = _merge_call(
            x2, oa, ob, oc, wg, w_branch_a[l].astype(BF16), wb, wc, w_out[l].astype(BF16),
            ln1_g[l].reshape(1, D), ln1_b[l].reshape(1, D), wrh, wrl, br, alpha=alpha, tm=tm)
        gate, dest, slot_token, block_expert, n_active = _routing(logits, T, tm=tm)
        xs = jnp.take(x1b, slot_token, axis=0)
        y = _moe_call(block_expert, n_active, xs, w_gate_up, b_gate_up.reshape(depth, N_EXPERTS, 1, -1),
                      w_down, b_down.reshape(depth, N_EXPERTS, 1, -1), layer=l)
        yk = jnp.take(y, dest.T.reshape(-1), axis=0).reshape(TOP_K, T, D)
        x2 = _final_call(x1, yk, gate, ln2_g[l].reshape(1, D), ln2_b[l].reshape(1, D), alpha=alpha, tm=tm)
    return x2.reshape(B, S, D)
```

```python
import functools
import math

import jax
import jax.numpy as jnp
from jax import lax
from jax.experimental import pallas as pl
from jax.experimental.pallas import tpu as pltpu

F32 = jnp.float32
BF16 = jnp.bfloat16

LANE = 128
GRID_W = 64
HEAD_DIM = 64
NA_HEADS = 6
NA_WIN_H = 8
NA_WIN_W = 16
NA_W = NA_HEADS * HEAD_DIM
GQA_HEADS = 6
GQA_KV_HEADS = 2
GQA_REP = GQA_HEADS // GQA_KV_HEADS
MLA_HEADS = 4
MLA_Q_RANK = 384
MLA_KV_RANK = 256
MLA_NOPE = 64
MLA_ROPE = 32
MLA_V = 64
MLA_QK = MLA_NOPE + MLA_ROPE
ROPE_THETA = 10000.0
N_BRANCH = 3
N_EXPERTS = 32
TOP_K = 4
D_EXPERT = 1024
SWIGLU_LIMIT = 7.0
SWIGLU_ALPHA = 1.702
MOE_BLOCK = 512
LN_EPS = 1e-5
RMS_EPS = 1e-6
NEG_INF = -1e30
LOG2_E = math.log2(math.e)
VMEM_LIMIT = 56 * 1024 * 1024

_C_NAQ = 0
_C_NAK = _C_NAQ + NA_W
_C_NAV = _C_NAK + NA_W
_C_GQ = _C_NAV + NA_W
_C_GK = _C_GQ + GQA_HEADS * LANE
_C_GV = _C_GK + GQA_KV_HEADS * LANE
_C_CQ = _C_GV + GQA_KV_HEADS * LANE
_C_CKV = _C_CQ + MLA_Q_RANK
_C_KR = _C_CKV + MLA_KV_RANK
_C_END = _C_KR + LANE


def _cparams(sem):
    return pltpu.CompilerParams(dimension_semantics=sem, vmem_limit_bytes=VMEM_LIMIT)


def _dot(a, b):
    return jnp.dot(a, b, preferred_element_type=F32)


def _dot_nt(a, b):
    return lax.dot_general(a, b, (((1,), (1,)), ((), ())), preferred_element_type=F32)


def _qkv_kernel(x_ref, w_ref, cs_ref, mq_ref, mk_ref, gq_ref, gk_ref, nq_ref, nkv_ref,
                wuq_ref, wuk_ref, wuv_ref,
                naq_ref, nak_ref, nav_ref, gqo_ref, gko_ref, gvo_ref, mqo_ref, mko_ref, mvo_ref, h_scr):
    tm = x_ref.shape[0]
    h_scr[...] = _dot(x_ref[...].astype(BF16), w_ref[...])

    def proj(c0, width):
        return h_scr[:, c0:c0 + width]

    naq_ref[...] = proj(_C_NAQ, NA_W).astype(BF16)
    nak_ref[...] = proj(_C_NAK, NA_W).astype(BF16)
    nav_ref[...] = proj(_C_NAV, NA_W).astype(BF16)

    lane = lax.broadcasted_iota(jnp.int32, (tm, LANE), 1)
    cs = cs_ref[...]

    def norm_rope(hc, gain):
        r = lax.rsqrt(jnp.mean(hc * hc, axis=-1, keepdims=True) + RMS_EPS)
        a = hc * r * (gain * cs)
        return jnp.where(lane < HEAD_DIM, a + pltpu.roll(a, HEAD_DIM, 1), 0.0)

    gq_gain = gq_ref[...]
    for h in range(GQA_HEADS):
        hc = proj(_C_GQ + h * LANE, LANE)
        gqo_ref[:, h * LANE:(h + 1) * LANE] = norm_rope(hc, gq_gain).astype(BF16)
    gk_gain = gk_ref[...]
    for g in range(GQA_KV_HEADS):
        hc = proj(_C_GK + g * LANE, LANE)
        gko_ref[:, g * LANE:(g + 1) * LANE] = norm_rope(hc, gk_gain).astype(BF16)
        hv = proj(_C_GV + g * LANE, LANE)
        gvo_ref[:, g * LANE:(g + 1) * LANE] = jnp.where(lane == HEAD_DIM, 1.0, hv).astype(BF16)

    def rms(v, gain):
        r = lax.rsqrt(jnp.mean(v * v, axis=-1, keepdims=True) + RMS_EPS)
        return (v * r * gain).astype(BF16)

    def mla_rope(b):
        summed = b + pltpu.roll(b, LANE - MLA_ROPE, 1)
        return jnp.where(lane < MLA_NOPE, b, jnp.where(lane < MLA_QK, summed, 0.0))

    hq = _dot(rms(proj(_C_CQ, MLA_Q_RANK), nq_ref[...]), wuq_ref[...])
    mq = mq_ref[...]
    for h in range(MLA_HEADS):
        mqo_ref[:, h * LANE:(h + 1) * LANE] = mla_rope(hq[:, h * LANE:(h + 1) * LANE] * mq).astype(BF16)

    ckv = rms(proj(_C_CKV, MLA_KV_RANK), nkv_ref[...])
    hk = _dot(ckv, wuk_ref[...])
    hv = _dot(ckv, wuv_ref[...])
    kr = mla_rope(proj(_C_KR, LANE) * mk_ref[...])
    for h in range(MLA_HEADS):
        sl = slice(h * LANE, (h + 1) * LANE)
        mko_ref[:, sl] = (hk[:, sl] + kr).astype(BF16)
        mvo_ref[:, sl] = jnp.where(lane == MLA_V, 1.0, hv[:, sl]).astype(BF16)


def _qkv_call(x2, wqkv, cs, mq, mk, gq, gk, nq, nkv, wuq, wuk, wuv, *, seq, tm):
    T, D = x2.shape
    ns = seq // tm

    def rows(width):
        return pl.BlockSpec((tm, width), lambda i: (i, 0))

    def pos(width):
        return pl.BlockSpec((tm, width), lambda i: (i % ns, 0))

    def whole(a):
        return pl.BlockSpec(a.shape, lambda i: (0,) * a.ndim)

    widths = [NA_W, NA_W, NA_W, GQA_HEADS * LANE, GQA_KV_HEADS * LANE, GQA_KV_HEADS * LANE,
              MLA_HEADS * LANE, MLA_HEADS * LANE, MLA_HEADS * LANE]
    return pl.pallas_call(
        _qkv_kernel,
        grid=(T // tm,),
        in_specs=[rows(D), whole(wqkv), pos(LANE), pos(LANE), pos(LANE), whole(gq), whole(gk),
                  whole(nq), whole(nkv), whole(wuq), whole(wuk), whole(wuv)],
        out_specs=[rows(w) for w in widths],
        out_shape=[jax.ShapeDtypeStruct((T, w), BF16) for w in widths],
        scratch_shapes=[pltpu.VMEM((tm, _C_END), F32)],
        compiler_params=_cparams(("parallel",)),
        name="qkv_proj",
    )(x2, wqkv, cs, mq, mk, gq, gk, nq, nkv, wuq, wuk, wuv)


NA_ROWS_PER_STEP = 8
NA_ROWS_PER_ITER = 2


def _na_kernel(q_ref, k_ref, v_ref, bias_ref, o_ref, *, n_rows):
    j = pl.program_id(1)
    win = NA_WIN_H * GRID_W
    lane = lax.broadcasted_iota(jnp.int32, (GRID_W, LANE), 1)
    low = lane < HEAD_DIM

    def rows_body(a2, carry):
        work = []
        for rr in range(NA_ROWS_PER_ITER):
            a = a2 * NA_ROWS_PER_ITER + rr
            r = j * NA_ROWS_PER_STEP + a
            r0 = jnp.clip(r - NA_WIN_H // 2, 0, n_rows - NA_WIN_H)
            variant = r - r0
            qrow = pl.ds(pl.multiple_of(a * GRID_W, GRID_W), GRID_W)
            krow = pl.ds(pl.multiple_of(r0 * GRID_W, GRID_W), win)
            for pair in range(NA_HEADS // 2):
                cols = slice(pair * LANE, (pair + 1) * LANE)
                qp = q_ref[qrow, cols]
                kp = k_ref[krow, cols]
                for half in range(2):
                    qm = jnp.where(low if half == 0 else jnp.logical_not(low), qp, jnp.zeros_like(qp))
                    s = _dot_nt(qm, kp) + bias_ref[variant, 2 * pair + half]
                    work.append((qrow, krow, cols, half, s))
        probs = []
        for qrow, krow, cols, half, s in work:
            m = jnp.max(s, axis=-1, keepdims=True)
            p = jnp.exp(s - m)
            probs.append((p.astype(BF16), jnp.sum(p, axis=-1, keepdims=True)))
        outs = []
        for (qrow, krow, cols, half, _), (p, l) in zip(work, probs):
            outs.append(_dot(p, v_ref[krow, cols]) / l)
        for n in range(0, len(work), 2):
            qrow, _, cols, _, _ = work[n]
            o_ref[qrow, cols] = jnp.where(low, outs[n], outs[n + 1]).astype(BF16)
        return carry

    lax.fori_loop(0, NA_ROWS_PER_STEP // NA_ROWS_PER_ITER, rows_body, 0)


def _na_call(q, k, v, bias, *, batch, seq):
    n_rows = seq // GRID_W
    steps = n_rows // NA_ROWS_PER_STEP
    tq = NA_ROWS_PER_STEP * GRID_W
    return pl.pallas_call(
        functools.partial(_na_kernel, n_rows=n_rows),
        grid=(batch, steps),
        in_specs=[pl.BlockSpec((tq, NA_W), lambda b, j: (b * steps + j, 0)),
                  pl.BlockSpec((seq, NA_W), lambda b, j: (b, 0)),
                  pl.BlockSpec((seq, NA_W), lambda b, j: (b, 0)),
                  pl.BlockSpec(bias.shape, lambda b, j: (0, 0, 0, 0))],
        out_specs=pl.BlockSpec((tq, NA_W), lambda b, j: (b * steps + j, 0)),
        out_shape=jax.ShapeDtypeStruct(q.shape, BF16),
        compiler_params=_cparams(("parallel", "arbitrary")),
        name="na_attn",
    )(q, k, v, bias)


def _na_bias_table(rpb):
    cols = jnp.arange(GRID_W)
    c0 = jnp.clip(cols - NA_WIN_W // 2, 0, GRID_W - NA_WIN_W)
    in_win = (cols[None, :] >= c0[:, None]) & (cols[None, :] < c0[:, None] + NA_WIN_W)
    idx_c = jnp.clip(cols[None, :] - cols[:, None] + (NA_WIN_W - 1), 0, 2 * NA_WIN_W - 2)
    variant = jnp.arange(NA_WIN_H)
    idx_r = jnp.arange(NA_WIN_H)[None, :] - variant[:, None] + (NA_WIN_H - 1)
    b = rpb.astype(F32)[:, idx_r]
    b = b[..., idx_c]
    b = jnp.where(in_win[None, None, None], b, NEG_INF)
    b = b.transpose(1, 0, 3, 2, 4)
    return b.reshape(NA_WIN_H, NA_HEADS, GRID_W, NA_WIN_H * GRID_W)


def _flash_kernel(q_ref, k_ref, v_ref, o_ref, s0, s1, p0, p1, a0, a1, m_scr, acc_scr,
                  *, units, tu, tk, nk, sum_lane):
    s_slot, p_slot, a_slot = (s0, s1), (p0, p1), (a0, a1)
    n_units = len(units)
    m_scr[...] = jnp.full(m_scr.shape, NEG_INF, F32)
    acc_scr[...] = jnp.zeros(acc_scr.shape, F32)

    def chunk(ref, c):
        start = c * tk if isinstance(c, int) else pl.multiple_of(c * tk, tk)
        return ref[pl.ds(start, tk), :]

    def scores(c, slot):
        k = chunk(k_ref, c)
        for u, (r0, c0) in enumerate(units):
            s_slot[slot][u] = _dot_nt(q_ref[r0:r0 + tu, c0:c0 + LANE], k)

    def softmax(slot):
        for u in range(n_units):
            s = s_slot[slot][u]
            m_prev = m_scr[u]
            m_new = jnp.maximum(m_prev, jnp.max(s, axis=-1, keepdims=True))
            a_slot[slot][u] = jnp.exp2(m_prev - m_new)
            p_slot[slot][u] = jnp.exp2(s - m_new[:, :1]).astype(BF16)
            m_scr[u] = m_new

    def accumulate(c, slot):
        v = chunk(v_ref, c)
        for u in range(n_units):
            acc_scr[u] = a_slot[slot][u] * acc_scr[u] + _dot(p_slot[slot][u], v)

    scores(0, 0)
    scores(1, 1)
    softmax(0)

    def body(j, carry):
        c1 = 2 * j + 1
        scores(c1 + 1, 0)
        accumulate(c1 - 1, 0)
        softmax(1)
        scores(c1 + 2, 1)
        accumulate(c1, 1)
        softmax(0)
        return carry

    lax.fori_loop(0, (nk - 2) // 2, body, 0)
    accumulate(nk - 2, 0)
    softmax(1)
    accumulate(nk - 1, 1)
    for u, (r0, c0) in enumerate(units):
        acc = acc_scr[u]
        o_ref[r0:r0 + tu, c0:c0 + LANE] = (acc / acc[:, sum_lane:sum_lane + 1]).astype(BF16)


def _flash_call(q, k, v, *, batch, seq, kv_heads, units, tu, tk, sum_lane, name):
    rows = max(r0 for r0, _ in units) + tu
    width = max(c0 for _, c0 in units) + LANE
    nq = seq // rows
    nk = seq // tk
    assert nk >= 2 and nk % 2 == 0
    n_units = len(units)
    return pl.pallas_call(
        functools.partial(_flash_kernel, units=units, tu=tu, tk=tk, nk=nk, sum_lane=sum_lane),
        grid=(batch, kv_heads, nq),
        in_specs=[pl.BlockSpec((rows, width), lambda b, g, i: (b * nq + i, g)),
                  pl.BlockSpec((seq, LANE), lambda b, g, i: (b, g)),
                  pl.BlockSpec((seq, LANE), lambda b, g, i: (b, g))],
        out_specs=pl.BlockSpec((rows, width), lambda b, g, i: (b * nq + i, g)),
        out_shape=jax.ShapeDtypeStruct(q.shape, BF16),
        scratch_shapes=[pltpu.VMEM((n_units, tu, tk), F32), pltpu.VMEM((n_units, tu, tk), F32),
                        pltpu.VMEM((n_units, tu, tk), BF16), pltpu.VMEM((n_units, tu, tk), BF16),
                        pltpu.VMEM((n_units, tu, LANE), F32), pltpu.VMEM((n_units, tu, LANE), F32),
                        pltpu.VMEM((n_units, tu, LANE), F32), pltpu.VMEM((n_units, tu, LANE), F32)],
        compiler_params=_cparams(("parallel", "parallel", "arbitrary")),
        name=name,
    )(q, k, v)


def _layer_norm(z, g, b):
    mu = jnp.mean(z, axis=-1, keepdims=True)
    zc = z - mu
    var = jnp.mean(zc * zc, axis=-1, keepdims=True)
    return zc * lax.rsqrt(var + LN_EPS) * g + b


def _merge_kernel(x_ref, oa_ref, ob_ref, oc_ref, wg_ref, wa_ref, wb_ref, wc_ref, wo_ref,
                  lng_ref, lnb_ref, wrh_ref, wrl_ref, br_ref, x1_ref, x1b_ref, logit_ref, *, alpha):
    d = x_ref.shape[1]
    x = x_ref[...]
    xb = x.astype(BF16)
    mixed = None
    for i, (o_ref, w_ref) in enumerate(((oa_ref, wa_ref), (ob_ref, wb_ref), (oc_ref, wc_ref))):
        gate = jax.nn.sigmoid(_dot(xb, wg_ref[:, i * d:(i + 1) * d]))
        term = gate * _dot(o_ref[...], w_ref[...])
        mixed = term if mixed is None else mixed + term
    z = alpha * x + _dot(mixed.astype(BF16), wo_ref[...])
    x1 = _layer_norm(z, lng_ref[...], lnb_ref[...])
    x1_ref[...] = x1
    hi = x1.astype(BF16)
    lo = (x1 - hi.astype(F32)).astype(BF16)
    x1b_ref[...] = hi
    both = _dot(hi, wrl_ref[...])
    logit_ref[...] = both[:, :LANE] + both[:, LANE:] + _dot(lo, wrh_ref[...]) + br_ref[...]


def _merge_call(x2, oa, ob, oc, wg, wa, wb, wc, wo, lng, lnb, wrh, wrl, br, *, alpha, tm):
    T, D = x2.shape

    def rows(width):
        return pl.BlockSpec((tm, width), lambda i: (i, 0))

    def whole(a):
        return pl.BlockSpec(a.shape, lambda i: (0,) * a.ndim)

    return pl.pallas_call(
        functools.partial(_merge_kernel, alpha=alpha),
        grid=(T // tm,),
        in_specs=[rows(D), rows(oa.shape[1]), rows(ob.shape[1]), rows(oc.shape[1]),
                  whole(wg), whole(wa), whole(wb), whole(wc), whole(wo),
                  whole(lng), whole(lnb), whole(wrh), whole(wrl), whole(br)],
        out_specs=[rows(D), rows(D), rows(LANE)],
        out_shape=[jax.ShapeDtypeStruct((T, D), F32), jax.ShapeDtypeStruct((T, D), BF16),
                   jax.ShapeDtypeStruct((T, LANE), F32)],
        compiler_params=_cparams(("parallel",)),
        name="merge_ln_router",
    )(x2, oa, ob, oc, wg, wa, wb, wc, wo, lng, lnb, wrh, wrl, br)


def _moe_kernel(be_ref, na_ref, xs_ref, wgu_ref, bgu_ref, wd_ref, bd_ref, y_ref, wgu_bf, wd_bf):
    i = pl.program_id(0)

    @pl.when(i < na_ref[0])
    def _():
        @pl.when(jnp.logical_or(i == 0, be_ref[i] != be_ref[jnp.maximum(i - 1, 0)]))
        def _():
            wgu_bf[...] = wgu_ref[0, 0].astype(BF16)
            wd_bf[...] = wd_ref[0, 0].astype(BF16)

        h = _dot(xs_ref[...], wgu_bf[...]) + bgu_ref[0, 0]
        g = jnp.minimum(h[:, :D_EXPERT], SWIGLU_LIMIT)
        u = jnp.clip(h[:, D_EXPERT:], -SWIGLU_LIMIT, SWIGLU_LIMIT)
        a = g * jax.nn.sigmoid(SWIGLU_ALPHA * g) * (u + 1.0)
        y_ref[...] = (_dot(a.astype(BF16), wd_bf[...]) + bd_ref[0, 0]).astype(y_ref.dtype)


def _moe_call(block_expert, n_active, xs, wgu, bgu, wd, bd, *, layer):
    P, D = xs.shape
    n_blocks = P // MOE_BLOCK

    def blk(i, be, na):
        return (jnp.minimum(i, na[0] - 1), 0)

    def per_expert(i, be, na):
        return (layer, be[jnp.minimum(i, na[0] - 1)], 0, 0)

    grid_spec = pltpu.PrefetchScalarGridSpec(
        num_scalar_prefetch=2,
        grid=(n_blocks,),
        in_specs=[pl.BlockSpec((MOE_BLOCK, D), blk),
                  pl.BlockSpec((1, 1, D, 2 * D_EXPERT), per_expert),
                  pl.BlockSpec((1, 1, 1, 2 * D_EXPERT), per_expert),
                  pl.BlockSpec((1, 1, D_EXPERT, D), per_expert),
                  pl.BlockSpec((1, 1, 1, D), per_expert)],
        out_specs=pl.BlockSpec((MOE_BLOCK, D), blk),
        scratch_shapes=[pltpu.VMEM((D, 2 * D_EXPERT), BF16), pltpu.VMEM((D_EXPERT, D), BF16)],
    )
    return pl.pallas_call(
        _moe_kernel,
        grid_spec=grid_spec,
        out_shape=jax.ShapeDtypeStruct((P, D), BF16),
        compiler_params=_cparams(("arbitrary",)),
        name="moe_ffn",
    )(block_expert, n_active, xs, wgu, bgu, wd, bd)


def _final_kernel(x_ref, y_ref, gate_ref, lng_ref, lnb_ref, o_ref, *, alpha):
    gate = gate_ref[...]
    f = None
    for k in range(TOP_K):
        term = y_ref[k].astype(F32) * gate[:, k:k + 1]
        f = term if f is None else f + term
    o_ref[...] = _layer_norm(alpha * x_ref[...] + f, lng_ref[...], lnb_ref[...])


def _final_call(x1, yk, gate, lng, lnb, *, alpha, tm):
    T, D = x1.shape
    return pl.pallas_call(
        functools.partial(_final_kernel, alpha=alpha),
        grid=(T // tm,),
        in_specs=[pl.BlockSpec((tm, D), lambda i: (i, 0)),
                  pl.BlockSpec((TOP_K, tm, D), lambda i: (0, i, 0)),
                  pl.BlockSpec((tm, TOP_K), lambda i: (i, 0)),
                  pl.BlockSpec((1, D), lambda i: (0, 0)),
                  pl.BlockSpec((1, D), lambda i: (0, 0))],
        out_specs=pl.BlockSpec((tm, D), lambda i: (i, 0)),
        out_shape=jax.ShapeDtypeStruct((T, D), F32),
        compiler_params=_cparams(("parallel",)),
        name="combine_ln",
    )(x1, yk, gate, lng, lnb)


def _rot_half(w):
    half = w.shape[-1] // 2
    return jnp.concatenate([w[..., half:], w[..., :half]], axis=-1)


def _axial_tables(seq):
    def cos_sin(dim):
        quarter = dim // 4
        inv = ROPE_THETA ** (-jnp.arange(quarter, dtype=F32) / quarter)
        t = jnp.arange(seq)
        row = (t // GRID_W).astype(F32)
        col = (t % GRID_W).astype(F32)
        ang = jnp.concatenate([row[:, None] * inv, col[:, None] * inv], -1)
        return jnp.cos(ang), jnp.sin(ang)

    c64, s64 = cos_sin(HEAD_DIM)
    c32, s32 = cos_sin(MLA_ROPE)
    cs = jnp.concatenate([c64, c64, -s64, s64], -1)
    m = jnp.concatenate([jnp.ones((seq, MLA_NOPE), F32), c32, c32, -s32, s32], -1)
    return cs, m * (MLA_QK ** -0.5 * LOG2_E), m


def _layer_weights(w_in, gqa_q_norm, gqa_k_norm, w_uq, w_ukv, w_branch_b, w_branch_c):
    D = w_in.shape[0]
    widths = [NA_W, NA_W, NA_W, GQA_HEADS * HEAD_DIM, GQA_KV_HEADS * HEAD_DIM, GQA_KV_HEADS * HEAD_DIM,
              MLA_Q_RANK, MLA_KV_RANK, MLA_ROPE, N_BRANCH * D]
    offs = [0]
    for w in widths:
        offs.append(offs[-1] + w)
    na_q, na_k, na_v, g_q, g_k, g_v, c_q, c_kv, k_r, gates = [w_in[:, offs[i]:offs[i + 1]] for i in range(10)]

    def heads_with_rot(w, n):
        w = w.reshape(D, n, HEAD_DIM)
        return jnp.concatenate([w, _rot_half(w)], -1).reshape(D, n * LANE)

    def heads_padded(w, n, width):
        w = w.reshape(w.shape[0], n, width)
        return jnp.pad(w, ((0, 0), (0, 0), (0, LANE - width))).reshape(w.shape[0], n * LANE)

    kr_cols = jnp.concatenate([jnp.zeros((D, MLA_NOPE), F32), k_r, _rot_half(k_r)], -1)
    wqkv = jnp.concatenate([na_q * (HEAD_DIM ** -0.5), na_k, na_v,
                            heads_with_rot(g_q, GQA_HEADS), heads_with_rot(g_k, GQA_KV_HEADS),
                            heads_padded(g_v, GQA_KV_HEADS, HEAD_DIM), c_q, c_kv, kr_cols], -1).astype(BF16)

    gq = (jnp.concatenate([gqa_q_norm, _rot_half(gqa_q_norm)]) * (HEAD_DIM ** -0.5 * LOG2_E)).reshape(1, LANE)
    gk = jnp.concatenate([gqa_k_norm, _rot_half(gqa_k_norm)]).reshape(1, LANE)

    uq = w_uq.reshape(MLA_Q_RANK, MLA_HEADS, MLA_QK)
    uq_rope = uq[..., MLA_NOPE:]
    wuq = jnp.concatenate([uq, _rot_half(uq_rope)], -1).reshape(MLA_Q_RANK, MLA_HEADS * LANE).astype(BF16)
    ukv = w_ukv.reshape(MLA_KV_RANK, MLA_HEADS, MLA_NOPE + MLA_V)
    wuk = heads_padded(ukv[..., :MLA_NOPE].reshape(MLA_KV_RANK, -1), MLA_HEADS, MLA_NOPE).astype(BF16)
    wuv = heads_padded(ukv[..., MLA_NOPE:].reshape(MLA_KV_RANK, -1), MLA_HEADS, MLA_V).astype(BF16)

    def rows_padded(w, n, width):
        w = w.reshape(n, width, D)
        return jnp.pad(w, ((0, 0), (0, LANE - width), (0, 0))).reshape(n * LANE, D).astype(BF16)

    wb = rows_padded(w_branch_b, GQA_HEADS, HEAD_DIM)
    wc = rows_padded(w_branch_c, MLA_HEADS, MLA_V)
    return wqkv, gq, gk, wuq, wuk, wuv, gates.astype(BF16), wb, wc


def _route_kernel(logit_ref, tril_ref, gate_ref, dest_ref, cnt_ref, cnt_scr, run_scr):
    phase = pl.program_id(0)
    i = pl.program_id(1)
    tm = logit_ref.shape[0]
    lane = lax.broadcasted_iota(jnp.int32, (tm, LANE), 1).astype(F32)

    @pl.when(jnp.logical_and(phase == 0, i == 0))
    def _():
        cnt_scr[...] = jnp.zeros(cnt_scr.shape, F32)
        run_scr[...] = jnp.zeros(run_scr.shape, F32)

    logits = logit_ref[...]
    sel = jnp.zeros((tm, LANE), F32)
    vals, hits = [], []
    for _ in range(TOP_K):
        m = jnp.max(logits, axis=-1, keepdims=True)
        idx = jnp.min(jnp.where(logits == m, lane, float(LANE)), axis=-1, keepdims=True)
        hit = lane == idx
        sel = jnp.where(hit, 1.0, sel)
        logits = jnp.where(hit, -jnp.inf, logits)
        vals.append(m)
        hits.append(hit)
    incl = _dot(tril_ref[...], sel.astype(BF16))

    @pl.when(phase == 0)
    def _():
        cnt_scr[...] = cnt_scr[...] + incl[tm - 1:tm, :]

    @pl.when(phase == 1)
    def _():
        counts = cnt_scr[...]
        padded = jnp.floor((counts + (MOE_BLOCK - 1)) * (1.0 / MOE_BLOCK)) * MOE_BLOCK
        lane8 = lax.broadcasted_iota(jnp.int32, counts.shape, 1)
        cum = padded
        shift = 1
        while shift < LANE:
            cum = cum + jnp.where(lane8 >= shift, pltpu.roll(cum, shift, 1), 0.0)
            shift *= 2
        starts = (cum - padded) + run_scr[...]
        base = starts[0:1, :] + (incl - sel)
        es = [jnp.exp(v - vals[0]) for v in vals]
        denom = es[0] + es[1] + es[2] + es[3]
        gate_out = jnp.zeros((tm, LANE), F32)
        dest_out = jnp.zeros((tm, LANE), F32)
        for k in range(TOP_K):
            d_k = jnp.sum(jnp.where(hits[k], base, 0.0), axis=-1, keepdims=True)
            gate_out = jnp.where(lane == float(k), es[k] / denom, gate_out)
            dest_out = jnp.where(lane == float(k), d_k, dest_out)
        gate_ref[...] = gate_out
        dest_ref[...] = dest_out.astype(jnp.int32)
        run_scr[...] = run_scr[...] + incl[tm - 1:tm, :]
        cnt_ref[...] = counts[0:1, :]


def _route_call(logits, *, tm):
    T = logits.shape[0]
    nt = T // tm
    tril = (jnp.arange(tm)[:, None] >= jnp.arange(tm)[None, :]).astype(BF16)
    return pl.pallas_call(
        _route_kernel,
        grid=(2, nt),
        in_specs=[pl.BlockSpec((tm, LANE), lambda p, i: (i, 0)),
                  pl.BlockSpec((tm, tm), lambda p, i: (0, 0))],
        out_specs=[pl.BlockSpec((tm, LANE), lambda p, i: (p * i, 0)),
                   pl.BlockSpec((tm, LANE), lambda p, i: (p * i, 0)),
                   pl.BlockSpec((1, LANE), lambda p, i: (0, 0))],
        out_shape=[jax.ShapeDtypeStruct((T, LANE), F32), jax.ShapeDtypeStruct((T, LANE), jnp.int32),
                   jax.ShapeDtypeStruct((1, LANE), F32)],
        scratch_shapes=[pltpu.VMEM((8, LANE), F32), pltpu.VMEM((8, LANE), F32)],
        compiler_params=_cparams(("arbitrary", "arbitrary")),
        name="route",
    )(logits, tril)


def _routing(logits, n_tokens, *, tm):
    gate, dest, cnt = _route_call(logits, tm=tm)
    gate = gate[:, :TOP_K]
    dest = dest[:, :TOP_K]
    counts = cnt[0, :N_EXPERTS].astype(jnp.int32)
    padded = ((counts + MOE_BLOCK - 1) // MOE_BLOCK) * MOE_BLOCK
    cum_padded = jnp.cumsum(padded)
    starts_padded = cum_padded - padded
    starts_sorted = jnp.cumsum(counts) - counts
    n_blocks = n_tokens * TOP_K // MOE_BLOCK + N_EXPERTS
    block_start = jnp.arange(n_blocks, dtype=jnp.int32) * MOE_BLOCK
    block_expert = jnp.minimum(jnp.sum((block_start[:, None] >= cum_padded[None, :]).astype(jnp.int32), axis=1),
                               N_EXPERTS - 1)
    n_active = (cum_padded[-1] // MOE_BLOCK).astype(jnp.int32).reshape(1)
    tok = jnp.broadcast_to(jnp.arange(n_tokens, dtype=jnp.int32)[:, None], dest.shape)
    _, tok_sorted = lax.sort_key_val(dest.reshape(-1), tok.reshape(-1))
    per_expert = jnp.stack([starts_padded - starts_sorted, starts_padded, counts], axis=1)
    per_block = _gather_rows(per_expert, block_expert)
    offs = jnp.arange(MOE_BLOCK, dtype=jnp.int32)[None, :]
    slot = block_start[:, None] + offs
    src = jnp.clip(slot - per_block[:, 0:1], 0, n_tokens * TOP_K - 1)
    valid = (slot - per_block[:, 1:2]) < per_block[:, 2:3]
    slot_token = jnp.where(valid, _gather_rows(tok_sorted, src), slot % n_tokens).reshape(-1)
    return gate, dest, slot_token, block_expert, n_active


def _gather_rows(a, idx):
    return a.at[idx].get(mode="promise_in_bounds")


def kernel(x, w_in, na_rpb, gqa_q_norm, gqa_k_norm, mla_q_norm, mla_kv_norm, w_uq, w_ukv, w_branch_a, w_branch_b, w_branch_c, w_out, ln1_g, ln1_b, w_router, b_router, w_gate_up, b_gate_up, w_down, b_down, ln2_g, ln2_b):
    B, S, D = x.shape
    T = B * S
    depth = w_in.shape[0]
    alpha = (2.0 * depth) ** 0.25
    tm = min(512, S)
    tq = min(512, S)
    tk = min(1024, S // 2)
    cs, mq_tab, mk_tab = _axial_tables(S)
    x2 = x.reshape(T, D)
    for l in range(depth):
        wqkv, gq, gk, wuq, wuk, wuv, wg, wb, wc = _layer_weights(
            w_in[l], gqa_q_norm[l], gqa_k_norm[l], w_uq[l], w_ukv[l], w_branch_b[l], w_branch_c[l])
        naq, nak, nav, gqo, gko, gvo, mqo, mko, mvo = _qkv_call(
            x2, wqkv, cs, mq_tab, mk_tab, gq, gk, mla_q_norm[l].reshape(1, -1), mla_kv_norm[l].reshape(1, -1),
            wuq, wuk, wuv, seq=S, tm=tm)
        oa = _na_call(naq, nak, nav, _na_bias_table(na_rpb[l]), batch=B, seq=S)
        ob = _flash_call(gqo, gko, gvo, batch=B, seq=S, kv_heads=GQA_KV_HEADS,
                         units=tuple((0, r * LANE) for r in range(GQA_REP)),
                         tu=tq, tk=tk, sum_lane=HEAD_DIM, name="gqa_attn")
        oc = _flash_call(mqo, mko, mvo, batch=B, seq=S, kv_heads=MLA_HEADS,
                         units=((0, 0), (tq, 0)), tu=tq, tk=tk, sum_lane=MLA_V, name="mla_attn")
        wr = jnp.pad(w_router[l], ((0, 0), (0, LANE - N_EXPERTS)))
        wrh = wr.astype(BF16)
        wrl = jnp.concatenate([wrh, (wr - wrh.astype(F32)).astype(BF16)], axis=1)
        br = jnp.pad(b_router[l], (0, LANE - N_EXPERTS), constant_values=NEG_INF).reshape(1, LANE)
        x1, x1b, logits = _merge_call(
            x2, oa, ob, oc, wg, w_branch_a[l].astype(BF16), wb, wc, w_out[l].astype(BF16),
            ln1_g[l].reshape(1, D), ln1_b[l].reshape(1, D), wrh, wrl, br, alpha=alpha, tm=tm)
        gate, dest, slot_token, block_expert, n_active = _routing(logits, T, tm=tm)
        xs = _gather_rows(x1b, slot_token)
        y = _moe_call(block_expert, n_active, xs, w_gate_up, b_gate_up.reshape(depth, N_EXPERTS, 1, -1),
                      w_down, b_down.reshape(depth, N_EXPERTS, 1, -1), layer=l)
        yk = _gather_rows(y, dest.T.reshape(-1)).reshape(TOP_K, T, D)
        x2 = _final_call(x1, yk, gate, ln2_g[l].reshape(1, D), ln2_b[l].reshape(1, D), alpha=alpha, tm=tm)
    return x2.reshape(B, S, D)
```

```python
import functools
import math

import jax
import jax.numpy as jnp
from jax import lax
from jax.experimental import pallas as pl
from jax.experimental.pallas import tpu as pltpu

F32 = jnp.float32
BF16 = jnp.bfloat16

LANE = 128
GRID_W = 64
HEAD_DIM = 64
NA_HEADS = 6
NA_WIN_H = 8
NA_WIN_W = 16
NA_W = NA_HEADS * HEAD_DIM
GQA_HEADS = 6
GQA_KV_HEADS = 2
GQA_REP = GQA_HEADS // GQA_KV_HEADS
MLA_HEADS = 4
MLA_Q_RANK = 384
MLA_KV_RANK = 256
MLA_NOPE = 64
MLA_ROPE = 32
MLA_V = 64
MLA_QK = MLA_NOPE + MLA_ROPE
ROPE_THETA = 10000.0
N_BRANCH = 3
N_EXPERTS = 32
TOP_K = 4
D_EXPERT = 1024
SWIGLU_LIMIT = 7.0
SWIGLU_ALPHA = 1.702
MOE_BLOCK = 512
LN_EPS = 1e-5
RMS_EPS = 1e-6
NEG_INF = -1e30
LOG2_E = math.log2(math.e)
VMEM_LIMIT = 56 * 1024 * 1024

_C_NAQ = 0
_C_NAK = _C_NAQ + NA_W
_C_NAV = _C_NAK + NA_W
_C_GQ = _C_NAV + NA_W
_C_GK = _C_GQ + GQA_HEADS * LANE
_C_GV = _C_GK + GQA_KV_HEADS * LANE
_C_CQ = _C_GV + GQA_KV_HEADS * LANE
_C_CKV = _C_CQ + MLA_Q_RANK
_C_KR = _C_CKV + MLA_KV_RANK
_C_END = _C_KR + LANE


def _cparams(sem):
    return pltpu.CompilerParams(dimension_semantics=sem, vmem_limit_bytes=VMEM_LIMIT)


def _dot(a, b):
    return jnp.dot(a, b, preferred_element_type=F32)


def _dot_nt(a, b):
    return lax.dot_general(a, b, (((1,), (1,)), ((), ())), preferred_element_type=F32)


def _qkv_kernel(x_ref, w_ref, cs_ref, mq_ref, mk_ref, gq_ref, gk_ref, nq_ref, nkv_ref,
                wuq_ref, wuk_ref, wuv_ref,
                naq_ref, nak_ref, nav_ref, gqo_ref, gko_ref, gvo_ref, mqo_ref, mko_ref, mvo_ref, h_scr):
    tm = x_ref.shape[0]
    h_scr[...] = _dot(x_ref[...].astype(BF16), w_ref[...])

    def proj(c0, width):
        return h_scr[:, c0:c0 + width]

    naq_ref[...] = proj(_C_NAQ, NA_W).astype(BF16)
    nak_ref[...] = proj(_C_NAK, NA_W).astype(BF16)
    nav_ref[...] = proj(_C_NAV, NA_W).astype(BF16)

    lane = lax.broadcasted_iota(jnp.int32, (tm, LANE), 1)
    cs = cs_ref[...]

    def norm_rope(hc, gain):
        r = lax.rsqrt(jnp.mean(hc * hc, axis=-1, keepdims=True) + RMS_EPS)
        a = hc * r * (gain * cs)
        return jnp.where(lane < HEAD_DIM, a + pltpu.roll(a, HEAD_DIM, 1), 0.0)

    gq_gain = gq_ref[...]
    for h in range(GQA_HEADS):
        hc = proj(_C_GQ + h * LANE, LANE)
        gqo_ref[:, h * LANE:(h + 1) * LANE] = norm_rope(hc, gq_gain).astype(BF16)
    gk_gain = gk_ref[...]
    for g in range(GQA_KV_HEADS):
        hc = proj(_C_GK + g * LANE, LANE)
        gko_ref[:, g * LANE:(g + 1) * LANE] = norm_rope(hc, gk_gain).astype(BF16)
        hv = proj(_C_GV + g * LANE, LANE)
        gvo_ref[:, g * LANE:(g + 1) * LANE] = jnp.where(lane == HEAD_DIM, 1.0, hv).astype(BF16)

    def rms(v, gain):
        r = lax.rsqrt(jnp.mean(v * v, axis=-1, keepdims=True) + RMS_EPS)
        return (v * r * gain).astype(BF16)

    def mla_rope(b):
        summed = b + pltpu.roll(b, LANE - MLA_ROPE, 1)
        return jnp.where(lane < MLA_NOPE, b, jnp.where(lane < MLA_QK, summed, 0.0))

    hq = _dot(rms(proj(_C_CQ, MLA_Q_RANK), nq_ref[...]), wuq_ref[...])
    mq = mq_ref[...]
    for h in range(MLA_HEADS):
        mqo_ref[:, h * LANE:(h + 1) * LANE] = mla_rope(hq[:, h * LANE:(h + 1) * LANE] * mq).astype(BF16)

    ckv = rms(proj(_C_CKV, MLA_KV_RANK), nkv_ref[...])
    hk = _dot(ckv, wuk_ref[...])
    hv = _dot(ckv, wuv_ref[...])
    kr = mla_rope(proj(_C_KR, LANE) * mk_ref[...])
    for h in range(MLA_HEADS):
        sl = slice(h * LANE, (h + 1) * LANE)
        mko_ref[:, sl] = (hk[:, sl] + kr).astype(BF16)
        mvo_ref[:, sl] = jnp.where(lane == MLA_V, 1.0, hv[:, sl]).astype(BF16)


def _qkv_call(x2, wqkv, cs, mq, mk, gq, gk, nq, nkv, wuq, wuk, wuv, *, seq, tm):
    T, D = x2.shape
    ns = seq // tm

    def rows(width):
        return pl.BlockSpec((tm, width), lambda i: (i, 0))

    def pos(width):
        return pl.BlockSpec((tm, width), lambda i: (i % ns, 0))

    def whole(a):
        return pl.BlockSpec(a.shape, lambda i: (0,) * a.ndim)

    widths = [NA_W, NA_W, NA_W, GQA_HEADS * LANE, GQA_KV_HEADS * LANE, GQA_KV_HEADS * LANE,
              MLA_HEADS * LANE, MLA_HEADS * LANE, MLA_HEADS * LANE]
    return pl.pallas_call(
        _qkv_kernel,
        grid=(T // tm,),
        in_specs=[rows(D), whole(wqkv), pos(LANE), pos(LANE), pos(LANE), whole(gq), whole(gk),
                  whole(nq), whole(nkv), whole(wuq), whole(wuk), whole(wuv)],
        out_specs=[rows(w) for w in widths],
        out_shape=[jax.ShapeDtypeStruct((T, w), BF16) for w in widths],
        scratch_shapes=[pltpu.VMEM((tm, _C_END), F32)],
        compiler_params=_cparams(("parallel",)),
        name="qkv_proj",
    )(x2, wqkv, cs, mq, mk, gq, gk, nq, nkv, wuq, wuk, wuv)


NA_ROWS_PER_STEP = 8
NA_ROWS_PER_ITER = 2


def _na_kernel(q_ref, k_ref, v_ref, bias_ref, o_ref, *, n_rows):
    j = pl.program_id(1)
    win = NA_WIN_H * GRID_W
    lane = lax.broadcasted_iota(jnp.int32, (GRID_W, LANE), 1)
    low = lane < HEAD_DIM

    def rows_body(a2, carry):
        work = []
        for rr in range(NA_ROWS_PER_ITER):
            a = a2 * NA_ROWS_PER_ITER + rr
            r = j * NA_ROWS_PER_STEP + a
            r0 = jnp.clip(r - NA_WIN_H // 2, 0, n_rows - NA_WIN_H)
            variant = r - r0
            qrow = pl.ds(pl.multiple_of(a * GRID_W, GRID_W), GRID_W)
            krow = pl.ds(pl.multiple_of(r0 * GRID_W, GRID_W), win)
            for pair in range(NA_HEADS // 2):
                cols = slice(pair * LANE, (pair + 1) * LANE)
                qp = q_ref[qrow, cols]
                kp = k_ref[krow, cols]
                for half in range(2):
                    qm = jnp.where(low if half == 0 else jnp.logical_not(low), qp, jnp.zeros_like(qp))
                    s = _dot_nt(qm, kp) + bias_ref[variant, 2 * pair + half]
                    work.append((qrow, krow, cols, half, s))
        probs = []
        for qrow, krow, cols, half, s in work:
            m = jnp.max(s, axis=-1, keepdims=True)
            p = jnp.exp(s - m)
            probs.append((p.astype(BF16), jnp.sum(p, axis=-1, keepdims=True)))
        outs = []
        for (qrow, krow, cols, half, _), (p, l) in zip(work, probs):
            outs.append(_dot(p, v_ref[krow, cols]) / l)
        for n in range(0, len(work), 2):
            qrow, _, cols, _, _ = work[n]
            o_ref[qrow, cols] = jnp.where(low, outs[n], outs[n + 1]).astype(BF16)
        return carry

    lax.fori_loop(0, NA_ROWS_PER_STEP // NA_ROWS_PER_ITER, rows_body, 0)


def _na_call(q, k, v, bias, *, batch, seq):
    n_rows = seq // GRID_W
    steps = n_rows // NA_ROWS_PER_STEP
    tq = NA_ROWS_PER_STEP * GRID_W
    return pl.pallas_call(
        functools.partial(_na_kernel, n_rows=n_rows),
        grid=(batch, steps),
        in_specs=[pl.BlockSpec((tq, NA_W), lambda b, j: (b * steps + j, 0)),
                  pl.BlockSpec((seq, NA_W), lambda b, j: (b, 0)),
                  pl.BlockSpec((seq, NA_W), lambda b, j: (b, 0)),
                  pl.BlockSpec(bias.shape, lambda b, j: (0, 0, 0, 0))],
        out_specs=pl.BlockSpec((tq, NA_W), lambda b, j: (b * steps + j, 0)),
        out_shape=jax.ShapeDtypeStruct(q.shape, BF16),
        compiler_params=_cparams(("parallel", "arbitrary")),
        name="na_attn",
    )(q, k, v, bias)


def _na_bias_table(rpb):
    cols = jnp.arange(GRID_W)
    c0 = jnp.clip(cols - NA_WIN_W // 2, 0, GRID_W - NA_WIN_W)
    in_win = (cols[None, :] >= c0[:, None]) & (cols[None, :] < c0[:, None] + NA_WIN_W)
    idx_c = jnp.clip(cols[None, :] - cols[:, None] + (NA_WIN_W - 1), 0, 2 * NA_WIN_W - 2)
    variant = jnp.arange(NA_WIN_H)
    idx_r = jnp.arange(NA_WIN_H)[None, :] - variant[:, None] + (NA_WIN_H - 1)
    b = rpb.astype(F32)[:, idx_r]
    b = b[..., idx_c]
    b = jnp.where(in_win[None, None, None], b, NEG_INF)
    b = b.transpose(1, 0, 3, 2, 4)
    return b.reshape(NA_WIN_H, NA_HEADS, GRID_W, NA_WIN_H * GRID_W)


def _flash_kernel(q_ref, k_ref, v_ref, o_ref, s0, s1, p0, p1, a0, a1, m_scr, acc_scr,
                  *, units, tu, tk, nk, sum_lane):
    s_slot, p_slot, a_slot = (s0, s1), (p0, p1), (a0, a1)
    n_units = len(units)
    m_scr[...] = jnp.full(m_scr.shape, NEG_INF, F32)
    acc_scr[...] = jnp.zeros(acc_scr.shape, F32)

    def chunk(ref, c):
        start = c * tk if isinstance(c, int) else pl.multiple_of(c * tk, tk)
        return ref[pl.ds(start, tk), :]

    def scores(c, slot):
        k = chunk(k_ref, c)
        for u, (r0, c0) in enumerate(units):
            s_slot[slot][u] = _dot_nt(q_ref[r0:r0 + tu, c0:c0 + LANE], k)

    def softmax(slot):
        for u in range(n_units):
            s = s_slot[slot][u]
            m_prev = m_scr[u]
            m_new = jnp.maximum(m_prev, jnp.max(s, axis=-1, keepdims=True))
            a_slot[slot][u] = jnp.exp2(m_prev - m_new)
            p_slot[slot][u] = jnp.exp2(s - m_new[:, :1]).astype(BF16)
            m_scr[u] = m_new

    def accumulate(c, slot):
        v = chunk(v_ref, c)
        for u in range(n_units):
            acc_scr[u] = a_slot[slot][u] * acc_scr[u] + _dot(p_slot[slot][u], v)

    scores(0, 0)
    scores(1, 1)
    softmax(0)

    def body(j, carry):
        c1 = 2 * j + 1
        scores(c1 + 1, 0)
        accumulate(c1 - 1, 0)
        softmax(1)
        scores(c1 + 2, 1)
        accumulate(c1, 1)
        softmax(0)
        return carry

    lax.fori_loop(0, (nk - 2) // 2, body, 0)
    accumulate(nk - 2, 0)
    softmax(1)
    accumulate(nk - 1, 1)
    for u, (r0, c0) in enumerate(units):
        acc = acc_scr[u]
        o_ref[r0:r0 + tu, c0:c0 + LANE] = (acc / acc[:, sum_lane:sum_lane + 1]).astype(BF16)


def _flash_call(q, k, v, *, batch, seq, kv_heads, units, tu, tk, sum_lane, name):
    rows = max(r0 for r0, _ in units) + tu
    width = max(c0 for _, c0 in units) + LANE
    nq = seq // rows
    nk = seq // tk
    assert nk >= 2 and nk % 2 == 0
    n_units = len(units)
    return pl.pallas_call(
        functools.partial(_flash_kernel, units=units, tu=tu, tk=tk, nk=nk, sum_lane=sum_lane),
        grid=(batch, kv_heads, nq),
        in_specs=[pl.BlockSpec((rows, width), lambda b, g, i: (b * nq + i, g)),
                  pl.BlockSpec((seq, LANE), lambda b, g, i: (b, g)),
                  pl.BlockSpec((seq, LANE), lambda b, g, i: (b, g))],
        out_specs=pl.BlockSpec((rows, width), lambda b, g, i: (b * nq + i, g)),
        out_shape=jax.ShapeDtypeStruct(q.shape, BF16),
        scratch_shapes=[pltpu.VMEM((n_units, tu, tk), F32), pltpu.VMEM((n_units, tu, tk), F32),
                        pltpu.VMEM((n_units, tu, tk), BF16), pltpu.VMEM((n_units, tu, tk), BF16),
                        pltpu.VMEM((n_units, tu, LANE), F32), pltpu.VMEM((n_units, tu, LANE), F32),
                        pltpu.VMEM((n_units, tu, LANE), F32), pltpu.VMEM((n_units, tu, LANE), F32)],
        compiler_params=_cparams(("parallel", "parallel", "arbitrary")),
        name=name,
    )(q, k, v)


def _layer_norm(z, g, b):
    mu = jnp.mean(z, axis=-1, keepdims=True)
    zc = z - mu
    var = jnp.mean(zc * zc, axis=-1, keepdims=True)
    return zc * lax.rsqrt(var + LN_EPS) * g + b


def _merge_kernel(x_ref, oa_ref, ob_ref, oc_ref, wg_ref, wa_ref, wb_ref, wc_ref, wo_ref,
                  lng_ref, lnb_ref, wrh_ref, wrl_ref, br_ref, x1_ref, x1b_ref, logit_ref, *, alpha):
    d = x_ref.shape[1]
    x = x_ref[...]
    xb = x.astype(BF16)
    mixed = None
    for i, (o_ref, w_ref) in enumerate(((oa_ref, wa_ref), (ob_ref, wb_ref), (oc_ref, wc_ref))):
        gate = jax.nn.sigmoid(_dot(xb, wg_ref[:, i * d:(i + 1) * d]))
        term = gate * _dot(o_ref[...], w_ref[...])
        mixed = term if mixed is None else mixed + term
    z = alpha * x + _dot(mixed.astype(BF16), wo_ref[...])
    x1 = _layer_norm(z, lng_ref[...], lnb_ref[...])
    x1_ref[...] = x1
    hi = x1.astype(BF16)
    lo = (x1 - hi.astype(F32)).astype(BF16)
    x1b_ref[...] = hi
    both = _dot(hi, wrl_ref[...])
    logit_ref[...] = both[:, :LANE] + both[:, LANE:] + _dot(lo, wrh_ref[...]) + br_ref[...]


def _merge_call(x2, oa, ob, oc, wg, wa, wb, wc, wo, lng, lnb, wrh, wrl, br, *, alpha, tm):
    T, D = x2.shape

    def rows(width):
        return pl.BlockSpec((tm, width), lambda i: (i, 0))

    def whole(a):
        return pl.BlockSpec(a.shape, lambda i: (0,) * a.ndim)

    return pl.pallas_call(
        functools.partial(_merge_kernel, alpha=alpha),
        grid=(T // tm,),
        in_specs=[rows(D), rows(oa.shape[1]), rows(ob.shape[1]), rows(oc.shape[1]),
                  whole(wg), whole(wa), whole(wb), whole(wc), whole(wo),
                  whole(lng), whole(lnb), whole(wrh), whole(wrl), whole(br)],
        out_specs=[rows(D), rows(D), rows(LANE)],
        out_shape=[jax.ShapeDtypeStruct((T, D), F32), jax.ShapeDtypeStruct((T, D), BF16),
                   jax.ShapeDtypeStruct((T, LANE), F32)],
        compiler_params=_cparams(("parallel",)),
        name="merge_ln_router",
    )(x2, oa, ob, oc, wg, wa, wb, wc, wo, lng, lnb, wrh, wrl, br)


def _moe_kernel(be_ref, na_ref, xs_ref, wgu_ref, bgu_ref, wd_ref, bd_ref, y_ref, wgu_bf, wd_bf):
    i = pl.program_id(0)

    @pl.when(i >= na_ref[0])
    def _():
        y_ref[...] = jnp.zeros(y_ref.shape, y_ref.dtype)

    @pl.when(i < na_ref[0])
    def _():
        @pl.when(jnp.logical_or(i == 0, be_ref[i] != be_ref[jnp.maximum(i - 1, 0)]))
        def _():
            wgu_bf[...] = wgu_ref[0, 0].astype(BF16)
            wd_bf[...] = wd_ref[0, 0].astype(BF16)

        h = _dot(xs_ref[...], wgu_bf[...]) + bgu_ref[0, 0]
        g = jnp.minimum(h[:, :D_EXPERT], SWIGLU_LIMIT)
        u = jnp.clip(h[:, D_EXPERT:], -SWIGLU_LIMIT, SWIGLU_LIMIT)
        a = g * jax.nn.sigmoid(SWIGLU_ALPHA * g) * (u + 1.0)
        y_ref[...] = (_dot(a.astype(BF16), wd_bf[...]) + bd_ref[0, 0]).astype(y_ref.dtype)


def _moe_call(block_expert, n_active, xs, wgu, bgu, wd, bd, *, layer):
    P, D = xs.shape
    n_blocks = P // MOE_BLOCK

    def blk(i, be, na):
        return (jnp.minimum(i, na[0] - 1), 0)

    def per_expert(i, be, na):
        return (layer, be[jnp.minimum(i, na[0] - 1)], 0, 0)

    grid_spec = pltpu.PrefetchScalarGridSpec(
        num_scalar_prefetch=2,
        grid=(n_blocks,),
        in_specs=[pl.BlockSpec((MOE_BLOCK, D), blk),
                  pl.BlockSpec((1, 1, D, 2 * D_EXPERT), per_expert),
                  pl.BlockSpec((1, 1, 1, 2 * D_EXPERT), per_expert),
                  pl.BlockSpec((1, 1, D_EXPERT, D), per_expert),
                  pl.BlockSpec((1, 1, 1, D), per_expert)],
        out_specs=pl.BlockSpec((MOE_BLOCK, D), lambda i, be, na: (i, 0)),
        scratch_shapes=[pltpu.VMEM((D, 2 * D_EXPERT), BF16), pltpu.VMEM((D_EXPERT, D), BF16)],
    )
    return pl.pallas_call(
        _moe_kernel,
        grid_spec=grid_spec,
        out_shape=jax.ShapeDtypeStruct((P, D), BF16),
        compiler_params=_cparams(("arbitrary",)),
        name="moe_ffn",
    )(block_expert, n_active, xs, wgu, bgu, wd, bd)


def _final_kernel(x_ref, y_ref, gate_ref, lng_ref, lnb_ref, o_ref, *, alpha):
    gate = gate_ref[...]
    f = None
    for k in range(TOP_K):
        term = y_ref[k].astype(F32) * gate[:, k:k + 1]
        f = term if f is None else f + term
    o_ref[...] = _layer_norm(alpha * x_ref[...] + f, lng_ref[...], lnb_ref[...])


def _final_call(x1, yk, gate, lng, lnb, *, alpha, tm):
    T, D = x1.shape
    return pl.pallas_call(
        functools.partial(_final_kernel, alpha=alpha),
        grid=(T // tm,),
        in_specs=[pl.BlockSpec((tm, D), lambda i: (i, 0)),
                  pl.BlockSpec((TOP_K, tm, D), lambda i: (0, i, 0)),
                  pl.BlockSpec((tm, TOP_K), lambda i: (i, 0)),
                  pl.BlockSpec((1, D), lambda i: (0, 0)),
                  pl.BlockSpec((1, D), lambda i: (0, 0))],
        out_specs=pl.BlockSpec((tm, D), lambda i: (i, 0)),
        out_shape=jax.ShapeDtypeStruct((T, D), F32),
        compiler_params=_cparams(("parallel",)),
        name="combine_ln",
    )(x1, yk, gate, lng, lnb)


def _rot_half(w):
    half = w.shape[-1] // 2
    return jnp.concatenate([w[..., half:], w[..., :half]], axis=-1)


def _axial_tables(seq):
    def cos_sin(dim):
        quarter = dim // 4
        inv = ROPE_THETA ** (-jnp.arange(quarter, dtype=F32) / quarter)
        t = jnp.arange(seq)
        row = (t // GRID_W).astype(F32)
        col = (t % GRID_W).astype(F32)
        ang = jnp.concatenate([row[:, None] * inv, col[:, None] * inv], -1)
        return jnp.cos(ang), jnp.sin(ang)

    c64, s64 = cos_sin(HEAD_DIM)
    c32, s32 = cos_sin(MLA_ROPE)
    cs = jnp.concatenate([c64, c64, -s64, s64], -1)
    m = jnp.concatenate([jnp.ones((seq, MLA_NOPE), F32), c32, c32, -s32, s32], -1)
    return cs, m * (MLA_QK ** -0.5 * LOG2_E), m


def _layer_weights(w_in, gqa_q_norm, gqa_k_norm, w_uq, w_ukv, w_branch_b, w_branch_c):
    D = w_in.shape[0]
    widths = [NA_W, NA_W, NA_W, GQA_HEADS * HEAD_DIM, GQA_KV_HEADS * HEAD_DIM, GQA_KV_HEADS * HEAD_DIM,
              MLA_Q_RANK, MLA_KV_RANK, MLA_ROPE, N_BRANCH * D]
    offs = [0]
    for w in widths:
        offs.append(offs[-1] + w)
    na_q, na_k, na_v, g_q, g_k, g_v, c_q, c_kv, k_r, gates = [w_in[:, offs[i]:offs[i + 1]] for i in range(10)]

    def heads_with_rot(w, n):
        w = w.reshape(D, n, HEAD_DIM)
        return jnp.concatenate([w, _rot_half(w)], -1).reshape(D, n * LANE)

    def heads_padded(w, n, width):
        w = w.reshape(w.shape[0], n, width)
        return jnp.pad(w, ((0, 0), (0, 0), (0, LANE - width))).reshape(w.shape[0], n * LANE)

    kr_cols = jnp.concatenate([jnp.zeros((D, MLA_NOPE), F32), k_r, _rot_half(k_r)], -1)
    wqkv = jnp.concatenate([na_q * (HEAD_DIM ** -0.5), na_k, na_v,
                            heads_with_rot(g_q, GQA_HEADS), heads_with_rot(g_k, GQA_KV_HEADS),
                            heads_padded(g_v, GQA_KV_HEADS, HEAD_DIM), c_q, c_kv, kr_cols], -1).astype(BF16)

    gq = (jnp.concatenate([gqa_q_norm, _rot_half(gqa_q_norm)]) * (HEAD_DIM ** -0.5 * LOG2_E)).reshape(1, LANE)
    gk = jnp.concatenate([gqa_k_norm, _rot_half(gqa_k_norm)]).reshape(1, LANE)

    uq = w_uq.reshape(MLA_Q_RANK, MLA_HEADS, MLA_QK)
    uq_rope = uq[..., MLA_NOPE:]
    wuq = jnp.concatenate([uq, _rot_half(uq_rope)], -1).reshape(MLA_Q_RANK, MLA_HEADS * LANE).astype(BF16)
    ukv = w_ukv.reshape(MLA_KV_RANK, MLA_HEADS, MLA_NOPE + MLA_V)
    wuk = heads_padded(ukv[..., :MLA_NOPE].reshape(MLA_KV_RANK, -1), MLA_HEADS, MLA_NOPE).astype(BF16)
    wuv = heads_padded(ukv[..., MLA_NOPE:].reshape(MLA_KV_RANK, -1), MLA_HEADS, MLA_V).astype(BF16)

    def rows_padded(w, n, width):
        w = w.reshape(n, width, D)
        return jnp.pad(w, ((0, 0), (0, LANE - width), (0, 0))).reshape(n * LANE, D).astype(BF16)

    wb = rows_padded(w_branch_b, GQA_HEADS, HEAD_DIM)
    wc = rows_padded(w_branch_c, MLA_HEADS, MLA_V)
    return wqkv, gq, gk, wuq, wuk, wuv, gates.astype(BF16), wb, wc


def _route_kernel(logit_ref, tril_ref, gate_ref, dest_ref, cnt_ref, cnt_scr, run_scr):
    phase = pl.program_id(0)
    i = pl.program_id(1)
    tm = logit_ref.shape[0]
    lane = lax.broadcasted_iota(jnp.int32, (tm, LANE), 1).astype(F32)

    @pl.when(jnp.logical_and(phase == 0, i == 0))
    def _():
        cnt_scr[...] = jnp.zeros(cnt_scr.shape, F32)
        run_scr[...] = jnp.zeros(run_scr.shape, F32)

    logits = logit_ref[...]
    sel = jnp.zeros((tm, LANE), F32)
    vals, hits = [], []
    for _ in range(TOP_K):
        m = jnp.max(logits, axis=-1, keepdims=True)
        idx = jnp.min(jnp.where(logits == m, lane, float(LANE)), axis=-1, keepdims=True)
        hit = lane == idx
        sel = jnp.where(hit, 1.0, sel)
        logits = jnp.where(hit, -jnp.inf, logits)
        vals.append(m)
        hits.append(hit)
    incl = _dot(tril_ref[...], sel.astype(BF16))

    @pl.when(phase == 0)
    def _():
        cnt_scr[...] = cnt_scr[...] + incl[tm - 1:tm, :]

    @pl.when(phase == 1)
    def _():
        counts = cnt_scr[...]
        padded = jnp.floor((counts + (MOE_BLOCK - 1)) * (1.0 / MOE_BLOCK)) * MOE_BLOCK
        lane8 = lax.broadcasted_iota(jnp.int32, counts.shape, 1)
        cum = padded
        shift = 1
        while shift < LANE:
            cum = cum + jnp.where(lane8 >= shift, pltpu.roll(cum, shift, 1), 0.0)
            shift *= 2
        starts = (cum - padded) + run_scr[...]
        base = starts[0:1, :] + (incl - sel)
        es = [jnp.exp(v - vals[0]) for v in vals]
        denom = es[0] + es[1] + es[2] + es[3]
        gate_out = jnp.zeros((tm, LANE), F32)
        dest_out = jnp.zeros((tm, LANE), F32)
        for k in range(TOP_K):
            d_k = jnp.sum(jnp.where(hits[k], base, 0.0), axis=-1, keepdims=True)
            gate_out = jnp.where(lane == float(k), es[k] / denom, gate_out)
            dest_out = jnp.where(lane == float(k), d_k, dest_out)
        gate_ref[...] = gate_out
        dest_ref[...] = dest_out.astype(jnp.int32)
        run_scr[...] = run_scr[...] + incl[tm - 1:tm, :]
        cnt_ref[...] = counts[0:1, :]


def _route_call(logits, *, tm):
    T = logits.shape[0]
    nt = T // tm
    tril = (jnp.arange(tm)[:, None] >= jnp.arange(tm)[None, :]).astype(BF16)
    return pl.pallas_call(
        _route_kernel,
        grid=(2, nt),
        in_specs=[pl.BlockSpec((tm, LANE), lambda p, i: (i, 0)),
                  pl.BlockSpec((tm, tm), lambda p, i: (0, 0))],
        out_specs=[pl.BlockSpec((tm, LANE), lambda p, i: (p * i, 0)),
                   pl.BlockSpec((tm, LANE), lambda p, i: (p * i, 0)),
                   pl.BlockSpec((1, LANE), lambda p, i: (0, 0))],
        out_shape=[jax.ShapeDtypeStruct((T, LANE), F32), jax.ShapeDtypeStruct((T, LANE), jnp.int32),
                   jax.ShapeDtypeStruct((1, LANE), F32)],
        scratch_shapes=[pltpu.VMEM((8, LANE), F32), pltpu.VMEM((8, LANE), F32)],
        compiler_params=_cparams(("arbitrary", "arbitrary")),
        name="route",
    )(logits, tril)


def _routing(logits, n_tokens, *, tm):
    gate, dest, cnt = _route_call(logits, tm=tm)
    gate = gate[:, :TOP_K]
    dest = dest[:, :TOP_K]
    counts = cnt[0, :N_EXPERTS].astype(jnp.int32)
    padded = ((counts + MOE_BLOCK - 1) // MOE_BLOCK) * MOE_BLOCK
    cum_padded = jnp.cumsum(padded)
    starts_padded = cum_padded - padded
    starts_sorted = jnp.cumsum(counts) - counts
    n_blocks = n_tokens * TOP_K // MOE_BLOCK + N_EXPERTS
    block_start = jnp.arange(n_blocks, dtype=jnp.int32) * MOE_BLOCK
    block_expert = jnp.minimum(jnp.sum((block_start[:, None] >= cum_padded[None, :]).astype(jnp.int32), axis=1),
                               N_EXPERTS - 1)
    n_active = (cum_padded[-1] // MOE_BLOCK).astype(jnp.int32).reshape(1)
    tok = jnp.broadcast_to(jnp.arange(n_tokens, dtype=jnp.int32)[:, None], dest.shape)
    _, tok_sorted = lax.sort_key_val(dest.reshape(-1), tok.reshape(-1))
    per_expert = jnp.stack([starts_padded - starts_sorted, starts_padded, counts], axis=1)
    per_block = _gather_rows(per_expert, block_expert)
    offs = jnp.arange(MOE_BLOCK, dtype=jnp.int32)[None, :]
    slot = block_start[:, None] + offs
    src = jnp.clip(slot - per_block[:, 0:1], 0, n_tokens * TOP_K - 1)
    valid = (slot - per_block[:, 1:2]) < per_block[:, 2:3]
    slot_token = jnp.where(valid, _gather_rows(tok_sorted, src), slot % n_tokens).reshape(-1)
    return gate, dest, slot_token, block_expert, n_active


def _gather_rows(a, idx):
    return a.at[idx].get(mode="promise_in_bounds")


N_STREAMS = 2


def _layer(x2, prep, w_gate_up, bgu, w_down, bd, tables, *, layer, batch, seq, alpha):
    T, D = x2.shape
    tm = min(512, seq)
    tq = min(512, seq)
    tk = min(1024, seq // 2)
    cs, mq_tab, mk_tab = tables
    naq, nak, nav, gqo, gko, gvo, mqo, mko, mvo = _qkv_call(
        x2, prep["wqkv"], cs, mq_tab, mk_tab, prep["gq"], prep["gk"], prep["nq"], prep["nkv"],
        prep["wuq"], prep["wuk"], prep["wuv"], seq=seq, tm=tm)
    oa = _na_call(naq, nak, nav, prep["na_bias"], batch=batch, seq=seq)
    ob = _flash_call(gqo, gko, gvo, batch=batch, seq=seq, kv_heads=GQA_KV_HEADS,
                     units=tuple((0, r * LANE) for r in range(GQA_REP)),
                     tu=tq, tk=tk, sum_lane=HEAD_DIM, name="gqa_attn")
    oc = _flash_call(mqo, mko, mvo, batch=batch, seq=seq, kv_heads=MLA_HEADS,
                     units=((0, 0), (tq, 0)), tu=tq, tk=tk, sum_lane=MLA_V, name="mla_attn")
    x1, x1b, logits = _merge_call(
        x2, oa, ob, oc, prep["wg"], prep["wa"], prep["wb"], prep["wc"], prep["wo"],
        prep["ln1_g"], prep["ln1_b"], prep["wrh"], prep["wrl"], prep["br"], alpha=alpha, tm=tm)
    gate, dest, slot_token, block_expert, n_active = _routing(logits, T, tm=tm)
    xs = _gather_rows(x1b, slot_token)
    y = _moe_call(block_expert, n_active, xs, w_gate_up, bgu, w_down, bd, layer=layer)
    yk = _gather_rows(y, dest.T.reshape(-1)).reshape(TOP_K, T, D)
    return _final_call(x1, yk, gate, prep["ln2_g"], prep["ln2_b"], alpha=alpha, tm=tm)


def kernel(x, w_in, na_rpb, gqa_q_norm, gqa_k_norm, mla_q_norm, mla_kv_norm, w_uq, w_ukv, w_branch_a, w_branch_b, w_branch_c, w_out, ln1_g, ln1_b, w_router, b_router, w_gate_up, b_gate_up, w_down, b_down, ln2_g, ln2_b):
    B, S, D = x.shape
    depth = w_in.shape[0]
    alpha = (2.0 * depth) ** 0.25
    tables = _axial_tables(S)
    preps = []
    for l in range(depth):
        wqkv, gq, gk, wuq, wuk, wuv, wg, wb, wc = _layer_weights(
            w_in[l], gqa_q_norm[l], gqa_k_norm[l], w_uq[l], w_ukv[l], w_branch_b[l], w_branch_c[l])
        wr = jnp.pad(w_router[l], ((0, 0), (0, LANE - N_EXPERTS)))
        wrh = wr.astype(BF16)
        preps.append(dict(
            wqkv=wqkv, gq=gq, gk=gk, wuq=wuq, wuk=wuk, wuv=wuv, wg=wg, wb=wb, wc=wc,
            nq=mla_q_norm[l].reshape(1, -1), nkv=mla_kv_norm[l].reshape(1, -1),
            na_bias=_na_bias_table(na_rpb[l]), wa=w_branch_a[l].astype(BF16), wo=w_out[l].astype(BF16),
            ln1_g=ln1_g[l].reshape(1, D), ln1_b=ln1_b[l].reshape(1, D),
            ln2_g=ln2_g[l].reshape(1, D), ln2_b=ln2_b[l].reshape(1, D),
            wrh=wrh, wrl=jnp.concatenate([wrh, (wr - wrh.astype(F32)).astype(BF16)], axis=1),
            br=jnp.pad(b_router[l], (0, LANE - N_EXPERTS), constant_values=NEG_INF).reshape(1, LANE)))
    bgu = b_gate_up.reshape(depth, N_EXPERTS, 1, -1)
    bd = b_down.reshape(depth, N_EXPERTS, 1, -1)
    n_streams = N_STREAMS if B % N_STREAMS == 0 else 1
    bs = B // n_streams
    outs = []
    for g in range(n_streams):
        x2 = x[g * bs:(g + 1) * bs].reshape(bs * S, D)
        for l in range(depth):
            x2 = _layer(x2, preps[l], w_gate_up, bgu, w_down, bd, tables,
                        layer=l, batch=bs, seq=S, alpha=alpha)
        outs.append(x2.reshape(bs, S, D))
    return outs[0] if n_streams == 1 else jnp.concatenate(outs, axis=0)
```

```python
import functools
import math

import jax
import jax.numpy as jnp
from jax import lax
from jax.experimental import pallas as pl
from jax.experimental.pallas import tpu as pltpu

F32 = jnp.float32
BF16 = jnp.bfloat16

LANE = 128
GRID_W = 64
HEAD_DIM = 64
NA_HEADS = 6
NA_WIN_H = 8
NA_WIN_W = 16
NA_W = NA_HEADS * HEAD_DIM
GQA_HEADS = 6
GQA_KV_HEADS = 2
GQA_REP = GQA_HEADS // GQA_KV_HEADS
MLA_HEADS = 4
MLA_Q_RANK = 384
MLA_KV_RANK = 256
MLA_NOPE = 64
MLA_ROPE = 32
MLA_V = 64
MLA_QK = MLA_NOPE + MLA_ROPE
ROPE_THETA = 10000.0
N_BRANCH = 3
N_EXPERTS = 32
TOP_K = 4
D_EXPERT = 1024
SWIGLU_LIMIT = 7.0
SWIGLU_ALPHA = 1.702
MOE_BLOCK = 512
LN_EPS = 1e-5
RMS_EPS = 1e-6
NEG_INF = -1e30
LOG2_E = math.log2(math.e)
VMEM_LIMIT = 56 * 1024 * 1024

_C_NAQ = 0
_C_NAK = _C_NAQ + NA_W
_C_NAV = _C_NAK + NA_W
_C_GQ = _C_NAV + NA_W
_C_GK = _C_GQ + GQA_HEADS * LANE
_C_GV = _C_GK + GQA_KV_HEADS * LANE
_C_CQ = _C_GV + GQA_KV_HEADS * LANE
_C_CKV = _C_CQ + MLA_Q_RANK
_C_KR = _C_CKV + MLA_KV_RANK
_C_END = _C_KR + LANE


def _cparams(sem):
    return pltpu.CompilerParams(dimension_semantics=sem, vmem_limit_bytes=VMEM_LIMIT)


def _dot(a, b):
    return jnp.dot(a, b, preferred_element_type=F32)


def _dot_nt(a, b):
    return lax.dot_general(a, b, (((1,), (1,)), ((), ())), preferred_element_type=F32)


def _qkv_kernel(x_ref, w_ref, cs_ref, mq_ref, mk_ref, gq_ref, gk_ref, nq_ref, nkv_ref,
                wuq_ref, wuk_ref, wuv_ref,
                naq_ref, nak_ref, nav_ref, gqo_ref, gko_ref, gvo_ref, mqo_ref, mko_ref, mvo_ref, h_scr):
    tm = x_ref.shape[0]
    h_scr[...] = _dot(x_ref[...].astype(BF16), w_ref[...])

    def proj(c0, width):
        return h_scr[:, c0:c0 + width]

    naq_ref[...] = proj(_C_NAQ, NA_W).astype(BF16)
    nak_ref[...] = proj(_C_NAK, NA_W).astype(BF16)
    nav_ref[...] = proj(_C_NAV, NA_W).astype(BF16)

    lane = lax.broadcasted_iota(jnp.int32, (tm, LANE), 1)
    cs = cs_ref[...]

    def norm_rope(hc, gain):
        r = lax.rsqrt(jnp.mean(hc * hc, axis=-1, keepdims=True) + RMS_EPS)
        a = hc * r * (gain * cs)
        return jnp.where(lane < HEAD_DIM, a + pltpu.roll(a, HEAD_DIM, 1), 0.0)

    gq_gain = gq_ref[...]
    for h in range(GQA_HEADS):
        hc = proj(_C_GQ + h * LANE, LANE)
        gqo_ref[:, h * LANE:(h + 1) * LANE] = norm_rope(hc, gq_gain).astype(BF16)
    gk_gain = gk_ref[...]
    for g in range(GQA_KV_HEADS):
        hc = proj(_C_GK + g * LANE, LANE)
        gko_ref[:, g * LANE:(g + 1) * LANE] = norm_rope(hc, gk_gain).astype(BF16)
        hv = proj(_C_GV + g * LANE, LANE)
        gvo_ref[:, g * LANE:(g + 1) * LANE] = jnp.where(lane == HEAD_DIM, 1.0, hv).astype(BF16)

    def rms(v, gain):
        r = lax.rsqrt(jnp.mean(v * v, axis=-1, keepdims=True) + RMS_EPS)
        return (v * r * gain).astype(BF16)

    def mla_rope(b):
        summed = b + pltpu.roll(b, LANE - MLA_ROPE, 1)
        return jnp.where(lane < MLA_NOPE, b, jnp.where(lane < MLA_QK, summed, 0.0))

    hq = _dot(rms(proj(_C_CQ, MLA_Q_RANK), nq_ref[...]), wuq_ref[...])
    mq = mq_ref[...]
    for h in range(MLA_HEADS):
        mqo_ref[:, h * LANE:(h + 1) * LANE] = mla_rope(hq[:, h * LANE:(h + 1) * LANE] * mq).astype(BF16)

    ckv = rms(proj(_C_CKV, MLA_KV_RANK), nkv_ref[...])
    hk = _dot(ckv, wuk_ref[...])
    hv = _dot(ckv, wuv_ref[...])
    kr = mla_rope(proj(_C_KR, LANE) * mk_ref[...])
    for h in range(MLA_HEADS):
        sl = slice(h * LANE, (h + 1) * LANE)
        mko_ref[:, sl] = (hk[:, sl] + kr).astype(BF16)
        mvo_ref[:, sl] = jnp.where(lane == MLA_V, 1.0, hv[:, sl]).astype(BF16)


def _qkv_call(x2, wqkv, cs, mq, mk, gq, gk, nq, nkv, wuq, wuk, wuv, *, seq, tm):
    T, D = x2.shape
    ns = seq // tm

    def rows(width):
        return pl.BlockSpec((tm, width), lambda i: (i, 0))

    def pos(width):
        return pl.BlockSpec((tm, width), lambda i: (i % ns, 0))

    def whole(a):
        return pl.BlockSpec(a.shape, lambda i: (0,) * a.ndim)

    widths = [NA_W, NA_W, NA_W, GQA_HEADS * LANE, GQA_KV_HEADS * LANE, GQA_KV_HEADS * LANE,
              MLA_HEADS * LANE, MLA_HEADS * LANE, MLA_HEADS * LANE]
    return pl.pallas_call(
        _qkv_kernel,
        grid=(T // tm,),
        in_specs=[rows(D), whole(wqkv), pos(LANE), pos(LANE), pos(LANE), whole(gq), whole(gk),
                  whole(nq), whole(nkv), whole(wuq), whole(wuk), whole(wuv)],
        out_specs=[rows(w) for w in widths],
        out_shape=[jax.ShapeDtypeStruct((T, w), BF16) for w in widths],
        scratch_shapes=[pltpu.VMEM((tm, _C_END), F32)],
        compiler_params=_cparams(("parallel",)),
        name="qkv_proj",
    )(x2, wqkv, cs, mq, mk, gq, gk, nq, nkv, wuq, wuk, wuv)


NA_ROWS_PER_STEP = 8
NA_ROWS_PER_ITER = 4


def _na_kernel(q_ref, k_ref, v_ref, bias_ref, o_ref, *, n_rows):
    j = pl.program_id(1)
    win = NA_WIN_H * GRID_W
    lane = lax.broadcasted_iota(jnp.int32, (GRID_W, LANE), 1)
    low = lane < HEAD_DIM

    def rows_body(a2, carry):
        work = []
        for rr in range(NA_ROWS_PER_ITER):
            a = a2 * NA_ROWS_PER_ITER + rr
            r = j * NA_ROWS_PER_STEP + a
            r0 = jnp.clip(r - NA_WIN_H // 2, 0, n_rows - NA_WIN_H)
            variant = r - r0
            qrow = pl.ds(pl.multiple_of(a * GRID_W, GRID_W), GRID_W)
            krow = pl.ds(pl.multiple_of(r0 * GRID_W, GRID_W), win)
            for pair in range(NA_HEADS // 2):
                cols = slice(pair * LANE, (pair + 1) * LANE)
                qp = q_ref[qrow, cols]
                kp = k_ref[krow, cols]
                for half in range(2):
                    qm = jnp.where(low if half == 0 else jnp.logical_not(low), qp, jnp.zeros_like(qp))
                    s = _dot_nt(qm, kp) + bias_ref[variant, 2 * pair + half]
                    work.append((qrow, krow, cols, half, s))
        probs = []
        for qrow, krow, cols, half, s in work:
            m = jnp.max(s, axis=-1, keepdims=True)
            p = jnp.exp(s - m)
            probs.append((p.astype(BF16), jnp.sum(p, axis=-1, keepdims=True)))
        outs = []
        for (qrow, krow, cols, half, _), (p, l) in zip(work, probs):
            outs.append(_dot(p, v_ref[krow, cols]) / l)
        for n in range(0, len(work), 2):
            qrow, _, cols, _, _ = work[n]
            o_ref[qrow, cols] = jnp.where(low, outs[n], outs[n + 1]).astype(BF16)
        return carry

    lax.fori_loop(0, NA_ROWS_PER_STEP // NA_ROWS_PER_ITER, rows_body, 0)


def _na_call(q, k, v, bias, *, batch, seq):
    n_rows = seq // GRID_W
    steps = n_rows // NA_ROWS_PER_STEP
    tq = NA_ROWS_PER_STEP * GRID_W
    return pl.pallas_call(
        functools.partial(_na_kernel, n_rows=n_rows),
        grid=(batch, steps),
        in_specs=[pl.BlockSpec((tq, NA_W), lambda b, j: (b * steps + j, 0)),
                  pl.BlockSpec((seq, NA_W), lambda b, j: (b, 0)),
                  pl.BlockSpec((seq, NA_W), lambda b, j: (b, 0)),
                  pl.BlockSpec(bias.shape, lambda b, j: (0, 0, 0, 0))],
        out_specs=pl.BlockSpec((tq, NA_W), lambda b, j: (b * steps + j, 0)),
        out_shape=jax.ShapeDtypeStruct(q.shape, BF16),
        compiler_params=_cparams(("parallel", "arbitrary")),
        name="na_attn",
    )(q, k, v, bias)


def _na_bias_table(rpb):
    cols = jnp.arange(GRID_W)
    c0 = jnp.clip(cols - NA_WIN_W // 2, 0, GRID_W - NA_WIN_W)
    in_win = (cols[None, :] >= c0[:, None]) & (cols[None, :] < c0[:, None] + NA_WIN_W)
    idx_c = jnp.clip(cols[None, :] - cols[:, None] + (NA_WIN_W - 1), 0, 2 * NA_WIN_W - 2)
    variant = jnp.arange(NA_WIN_H)
    idx_r = jnp.arange(NA_WIN_H)[None, :] - variant[:, None] + (NA_WIN_H - 1)
    b = rpb.astype(F32)[:, idx_r]
    b = b[..., idx_c]
    b = jnp.where(in_win[None, None, None], b, NEG_INF)
    b = b.transpose(1, 0, 3, 2, 4)
    return b.reshape(NA_WIN_H, NA_HEADS, GRID_W, NA_WIN_H * GRID_W)


def _flash_kernel(q_ref, k_ref, v_ref, o_ref, q_scr, s0, s1, p0, p1, a0, a1, m_scr, acc_scr,
                  *, units, tu, tk, nk, sum_lane):
    s_slot, p_slot, a_slot = (s0, s1), (p0, p1), (a0, a1)
    n_units = len(units)
    for u, (r0, c0) in enumerate(units):
        q_scr[u * tu:(u + 1) * tu, :] = q_ref[r0:r0 + tu, c0:c0 + LANE]
    m_scr[...] = jnp.full(m_scr.shape, NEG_INF, F32)
    acc_scr[...] = jnp.zeros(acc_scr.shape, F32)
    whole = (slice(0, n_units * tu),)
    per_unit = tuple(slice(u * tu, (u + 1) * tu) for u in range(n_units))

    def chunk(ref, c):
        start = c * tk if isinstance(c, int) else pl.multiple_of(c * tk, tk)
        return ref[pl.ds(start, tk), :]

    def scores(c, slot, parts):
        k = chunk(k_ref, c)
        for r in parts:
            s_slot[slot][r, :] = _dot_nt(q_scr[r, :], k)

    def softmax(slot, parts):
        for r in parts:
            s = s_slot[slot][r, :]
            m_prev = m_scr[r, :]
            m_new = jnp.maximum(m_prev, jnp.max(s, axis=-1, keepdims=True))
            a_slot[slot][r, :] = jnp.exp2(m_prev - m_new)
            p_slot[slot][r, :] = jnp.exp2(s - m_new[:, :1]).astype(BF16)
            m_scr[r, :] = m_new

    def accumulate(c, slot, parts):
        v = chunk(v_ref, c)
        for r in parts:
            acc_scr[r, :] = a_slot[slot][r, :] * acc_scr[r, :] + _dot(p_slot[slot][r, :], v)

    scores(0, 0, per_unit)
    scores(1, 1, per_unit)
    softmax(0, per_unit)

    def body(j, carry):
        c1 = 2 * j + 1
        scores(c1 + 1, 0, whole)
        accumulate(c1 - 1, 0, whole)
        softmax(1, whole)
        scores(c1 + 2, 1, whole)
        accumulate(c1, 1, whole)
        softmax(0, whole)
        return carry

    lax.fori_loop(0, (nk - 2) // 2, body, 0)
    accumulate(nk - 2, 0, per_unit)
    softmax(1, per_unit)
    accumulate(nk - 1, 1, per_unit)
    for u, (r0, c0) in enumerate(units):
        acc = acc_scr[u * tu:(u + 1) * tu, :]
        o_ref[r0:r0 + tu, c0:c0 + LANE] = (acc / acc[:, sum_lane:sum_lane + 1]).astype(BF16)


def _flash_call(q, k, v, *, batch, seq, kv_heads, units, tu, tk, sum_lane, name):
    rows = max(r0 for r0, _ in units) + tu
    width = max(c0 for _, c0 in units) + LANE
    nq = seq // rows
    nk = seq // tk
    assert nk >= 2 and nk % 2 == 0
    stacked = len(units) * tu
    return pl.pallas_call(
        functools.partial(_flash_kernel, units=units, tu=tu, tk=tk, nk=nk, sum_lane=sum_lane),
        grid=(batch, kv_heads, nq),
        in_specs=[pl.BlockSpec((rows, width), lambda b, g, i: (b * nq + i, g)),
                  pl.BlockSpec((seq, LANE), lambda b, g, i: (b, g)),
                  pl.BlockSpec((seq, LANE), lambda b, g, i: (b, g))],
        out_specs=pl.BlockSpec((rows, width), lambda b, g, i: (b * nq + i, g)),
        out_shape=jax.ShapeDtypeStruct(q.shape, BF16),
        scratch_shapes=[pltpu.VMEM((stacked, LANE), BF16),
                        pltpu.VMEM((stacked, tk), F32), pltpu.VMEM((stacked, tk), F32),
                        pltpu.VMEM((stacked, tk), BF16), pltpu.VMEM((stacked, tk), BF16),
                        pltpu.VMEM((stacked, LANE), F32), pltpu.VMEM((stacked, LANE), F32),
                        pltpu.VMEM((stacked, LANE), F32), pltpu.VMEM((stacked, LANE), F32)],
        compiler_params=_cparams(("parallel", "parallel", "arbitrary")),
        name=name,
    )(q, k, v)


def _layer_norm(z, g, b):
    mu = jnp.mean(z, axis=-1, keepdims=True)
    zc = z - mu
    var = jnp.mean(zc * zc, axis=-1, keepdims=True)
    return zc * lax.rsqrt(var + LN_EPS) * g + b


def _merge_kernel(x_ref, oa_ref, ob_ref, oc_ref, wg_ref, wa_ref, wb_ref, wc_ref, wo_ref,
                  lng_ref, lnb_ref, wrh_ref, wrl_ref, br_ref, x1_ref, x1b_ref, logit_ref, *, alpha):
    d = x_ref.shape[1]
    x = x_ref[...]
    xb = x.astype(BF16)
    mixed = None
    for i, (o_ref, w_ref) in enumerate(((oa_ref, wa_ref), (ob_ref, wb_ref), (oc_ref, wc_ref))):
        gate = jax.nn.sigmoid(_dot(xb, wg_ref[:, i * d:(i + 1) * d]))
        term = gate * _dot(o_ref[...], w_ref[...])
        mixed = term if mixed is None else mixed + term
    z = alpha * x + _dot(mixed.astype(BF16), wo_ref[...])
    x1 = _layer_norm(z, lng_ref[...], lnb_ref[...])
    x1_ref[...] = x1
    hi = x1.astype(BF16)
    lo = (x1 - hi.astype(F32)).astype(BF16)
    x1b_ref[...] = hi
    both = _dot(hi, wrl_ref[...])
    logit_ref[...] = both[:, :LANE] + both[:, LANE:] + _dot(lo, wrh_ref[...]) + br_ref[...]


def _merge_call(x2, oa, ob, oc, wg, wa, wb, wc, wo, lng, lnb, wrh, wrl, br, *, alpha, tm):
    T, D = x2.shape

    def rows(width):
        return pl.BlockSpec((tm, width), lambda i: (i, 0))

    def whole(a):
        return pl.BlockSpec(a.shape, lambda i: (0,) * a.ndim)

    return pl.pallas_call(
        functools.partial(_merge_kernel, alpha=alpha),
        grid=(T // tm,),
        in_specs=[rows(D), rows(oa.shape[1]), rows(ob.shape[1]), rows(oc.shape[1]),
                  whole(wg), whole(wa), whole(wb), whole(wc), whole(wo),
                  whole(lng), whole(lnb), whole(wrh), whole(wrl), whole(br)],
        out_specs=[rows(D), rows(D), rows(LANE)],
        out_shape=[jax.ShapeDtypeStruct((T, D), F32), jax.ShapeDtypeStruct((T, D), BF16),
                   jax.ShapeDtypeStruct((T, LANE), F32)],
        compiler_params=_cparams(("parallel",)),
        name="merge_ln_router",
    )(x2, oa, ob, oc, wg, wa, wb, wc, wo, lng, lnb, wrh, wrl, br)


def _moe_kernel(be_ref, na_ref, xs_ref, wgu_ref, bgu_ref, wd_ref, bd_ref, y_ref, wgu_bf, wd_bf):
    i = pl.program_id(0)

    @pl.when(i >= na_ref[0])
    def _():
        y_ref[...] = jnp.zeros(y_ref.shape, y_ref.dtype)

    @pl.when(i < na_ref[0])
    def _():
        @pl.when(jnp.logical_or(i == 0, be_ref[i] != be_ref[jnp.maximum(i - 1, 0)]))
        def _():
            wgu_bf[...] = wgu_ref[0, 0].astype(BF16)
            wd_bf[...] = wd_ref[0, 0].astype(BF16)

        h = _dot(xs_ref[...], wgu_bf[...]) + bgu_ref[0, 0]
        g = jnp.minimum(h[:, :D_EXPERT], SWIGLU_LIMIT)
        u = jnp.clip(h[:, D_EXPERT:], -SWIGLU_LIMIT, SWIGLU_LIMIT)
        a = g * jax.nn.sigmoid(SWIGLU_ALPHA * g) * (u + 1.0)
        y_ref[...] = (_dot(a.astype(BF16), wd_bf[...]) + bd_ref[0, 0]).astype(y_ref.dtype)


def _moe_call(block_expert, n_active, xs, wgu, bgu, wd, bd, *, layer):
    P, D = xs.shape
    n_blocks = P // MOE_BLOCK

    def blk(i, be, na):
        return (jnp.minimum(i, na[0] - 1), 0)

    def per_expert(i, be, na):
        return (layer, be[jnp.minimum(i, na[0] - 1)], 0, 0)

    grid_spec = pltpu.PrefetchScalarGridSpec(
        num_scalar_prefetch=2,
        grid=(n_blocks,),
        in_specs=[pl.BlockSpec((MOE_BLOCK, D), blk),
                  pl.BlockSpec((1, 1, D, 2 * D_EXPERT), per_expert),
                  pl.BlockSpec((1, 1, 1, 2 * D_EXPERT), per_expert),
                  pl.BlockSpec((1, 1, D_EXPERT, D), per_expert),
                  pl.BlockSpec((1, 1, 1, D), per_expert)],
        out_specs=pl.BlockSpec((MOE_BLOCK, D), lambda i, be, na: (i, 0)),
        scratch_shapes=[pltpu.VMEM((D, 2 * D_EXPERT), BF16), pltpu.VMEM((D_EXPERT, D), BF16)],
    )
    return pl.pallas_call(
        _moe_kernel,
        grid_spec=grid_spec,
        out_shape=jax.ShapeDtypeStruct((P, D), BF16),
        compiler_params=_cparams(("arbitrary",)),
        name="moe_ffn",
    )(block_expert, n_active, xs, wgu, bgu, wd, bd)


def _final_kernel(x_ref, y_ref, gate_ref, lng_ref, lnb_ref, o_ref, *, alpha):
    gate = gate_ref[...]
    f = None
    for k in range(TOP_K):
        term = y_ref[k].astype(F32) * gate[:, k:k + 1]
        f = term if f is None else f + term
    o_ref[...] = _layer_norm(alpha * x_ref[...] + f, lng_ref[...], lnb_ref[...])


def _final_call(x1, yk, gate, lng, lnb, *, alpha, tm):
    T, D = x1.shape
    return pl.pallas_call(
        functools.partial(_final_kernel, alpha=alpha),
        grid=(T // tm,),
        in_specs=[pl.BlockSpec((tm, D), lambda i: (i, 0)),
                  pl.BlockSpec((TOP_K, tm, D), lambda i: (0, i, 0)),
                  pl.BlockSpec((tm, TOP_K), lambda i: (i, 0)),
                  pl.BlockSpec((1, D), lambda i: (0, 0)),
                  pl.BlockSpec((1, D), lambda i: (0, 0))],
        out_specs=pl.BlockSpec((tm, D), lambda i: (i, 0)),
        out_shape=jax.ShapeDtypeStruct((T, D), F32),
        compiler_params=_cparams(("parallel",)),
        name="combine_ln",
    )(x1, yk, gate, lng, lnb)


def _rot_half(w):
    half = w.shape[-1] // 2
    return jnp.concatenate([w[..., half:], w[..., :half]], axis=-1)


def _axial_tables(seq):
    def cos_sin(dim):
        quarter = dim // 4
        inv = ROPE_THETA ** (-jnp.arange(quarter, dtype=F32) / quarter)
        t = jnp.arange(seq)
        row = (t // GRID_W).astype(F32)
        col = (t % GRID_W).astype(F32)
        ang = jnp.concatenate([row[:, None] * inv, col[:, None] * inv], -1)
        return jnp.cos(ang), jnp.sin(ang)

    c64, s64 = cos_sin(HEAD_DIM)
    c32, s32 = cos_sin(MLA_ROPE)
    cs = jnp.concatenate([c64, c64, -s64, s64], -1)
    m = jnp.concatenate([jnp.ones((seq, MLA_NOPE), F32), c32, c32, -s32, s32], -1)
    return cs, m * (MLA_QK ** -0.5 * LOG2_E), m


def _layer_weights(w_in, gqa_q_norm, gqa_k_norm, w_uq, w_ukv, w_branch_b, w_branch_c):
    D = w_in.shape[0]
    widths = [NA_W, NA_W, NA_W, GQA_HEADS * HEAD_DIM, GQA_KV_HEADS * HEAD_DIM, GQA_KV_HEADS * HEAD_DIM,
              MLA_Q_RANK, MLA_KV_RANK, MLA_ROPE, N_BRANCH * D]
    offs = [0]
    for w in widths:
        offs.append(offs[-1] + w)
    na_q, na_k, na_v, g_q, g_k, g_v, c_q, c_kv, k_r, gates = [w_in[:, offs[i]:offs[i + 1]] for i in range(10)]

    def heads_with_rot(w, n):
        w = w.reshape(D, n, HEAD_DIM)
        return jnp.concatenate([w, _rot_half(w)], -1).reshape(D, n * LANE)

    def heads_padded(w, n, width):
        w = w.reshape(w.shape[0], n, width)
        return jnp.pad(w, ((0, 0), (0, 0), (0, LANE - width))).reshape(w.shape[0], n * LANE)

    kr_cols = jnp.concatenate([jnp.zeros((D, MLA_NOPE), F32), k_r, _rot_half(k_r)], -1)
    wqkv = jnp.concatenate([na_q * (HEAD_DIM ** -0.5), na_k, na_v,
                            heads_with_rot(g_q, GQA_HEADS), heads_with_rot(g_k, GQA_KV_HEADS),
                            heads_padded(g_v, GQA_KV_HEADS, HEAD_DIM), c_q, c_kv, kr_cols], -1).astype(BF16)

    gq = (jnp.concatenate([gqa_q_norm, _rot_half(gqa_q_norm)]) * (HEAD_DIM ** -0.5 * LOG2_E)).reshape(1, LANE)
    gk = jnp.concatenate([gqa_k_norm, _rot_half(gqa_k_norm)]).reshape(1, LANE)

    uq = w_uq.reshape(MLA_Q_RANK, MLA_HEADS, MLA_QK)
    uq_rope = uq[..., MLA_NOPE:]
    wuq = jnp.concatenate([uq, _rot_half(uq_rope)], -1).reshape(MLA_Q_RANK, MLA_HEADS * LANE).astype(BF16)
    ukv = w_ukv.reshape(MLA_KV_RANK, MLA_HEADS, MLA_NOPE + MLA_V)
    wuk = heads_padded(ukv[..., :MLA_NOPE].reshape(MLA_KV_RANK, -1), MLA_HEADS, MLA_NOPE).astype(BF16)
    wuv = heads_padded(ukv[..., MLA_NOPE:].reshape(MLA_KV_RANK, -1), MLA_HEADS, MLA_V).astype(BF16)

    def rows_padded(w, n, width):
        w = w.reshape(n, width, D)
        return jnp.pad(w, ((0, 0), (0, LANE - width), (0, 0))).reshape(n * LANE, D).astype(BF16)

    wb = rows_padded(w_branch_b, GQA_HEADS, HEAD_DIM)
    wc = rows_padded(w_branch_c, MLA_HEADS, MLA_V)
    return wqkv, gq, gk, wuq, wuk, wuv, gates.astype(BF16), wb, wc


def _route_kernel(logit_ref, tril_ref, gate_ref, dest_ref, cnt_ref, cnt_scr, run_scr):
    phase = pl.program_id(0)
    i = pl.program_id(1)
    tm = logit_ref.shape[0]
    lane = lax.broadcasted_iota(jnp.int32, (tm, LANE), 1).astype(F32)

    @pl.when(jnp.logical_and(phase == 0, i == 0))
    def _():
        cnt_scr[...] = jnp.zeros(cnt_scr.shape, F32)
        run_scr[...] = jnp.zeros(run_scr.shape, F32)

    logits = logit_ref[...]
    sel = jnp.zeros((tm, LANE), F32)
    vals, hits = [], []
    for _ in range(TOP_K):
        m = jnp.max(logits, axis=-1, keepdims=True)
        idx = jnp.min(jnp.where(logits == m, lane, float(LANE)), axis=-1, keepdims=True)
        hit = lane == idx
        sel = jnp.where(hit, 1.0, sel)
        logits = jnp.where(hit, -jnp.inf, logits)
        vals.append(m)
        hits.append(hit)
    incl = _dot(tril_ref[...], sel.astype(BF16))

    @pl.when(phase == 0)
    def _():
        cnt_scr[...] = cnt_scr[...] + incl[tm - 1:tm, :]

    @pl.when(phase == 1)
    def _():
        counts = cnt_scr[...]
        padded = jnp.floor((counts + (MOE_BLOCK - 1)) * (1.0 / MOE_BLOCK)) * MOE_BLOCK
        lane8 = lax.broadcasted_iota(jnp.int32, counts.shape, 1)
        cum = padded
        shift = 1
        while shift < LANE:
            cum = cum + jnp.where(lane8 >= shift, pltpu.roll(cum, shift, 1), 0.0)
            shift *= 2
        starts = (cum - padded) + run_scr[...]
        base = starts[0:1, :] + (incl - sel)
        es = [jnp.exp(v - vals[0]) for v in vals]
        denom = es[0] + es[1] + es[2] + es[3]
        gate_out = jnp.zeros((tm, LANE), F32)
        dest_out = jnp.zeros((tm, LANE), F32)
        for k in range(TOP_K):
            d_k = jnp.sum(jnp.where(hits[k], base, 0.0), axis=-1, keepdims=True)
            gate_out = jnp.where(lane == float(k), es[k] / denom, gate_out)
            dest_out = jnp.where(lane == float(k), d_k, dest_out)
        gate_ref[...] = gate_out
        dest_ref[...] = dest_out.astype(jnp.int32)
        run_scr[...] = run_scr[...] + incl[tm - 1:tm, :]
        cnt_ref[...] = counts[0:1, :]


def _route_call(logits, *, tm):
    T = logits.shape[0]
    nt = T // tm
    tril = (jnp.arange(tm)[:, None] >= jnp.arange(tm)[None, :]).astype(BF16)
    return pl.pallas_call(
        _route_kernel,
        grid=(2, nt),
        in_specs=[pl.BlockSpec((tm, LANE), lambda p, i: (i, 0)),
                  pl.BlockSpec((tm, tm), lambda p, i: (0, 0))],
        out_specs=[pl.BlockSpec((tm, LANE), lambda p, i: (p * i, 0)),
                   pl.BlockSpec((tm, LANE), lambda p, i: (p * i, 0)),
                   pl.BlockSpec((1, LANE), lambda p, i: (0, 0))],
        out_shape=[jax.ShapeDtypeStruct((T, LANE), F32), jax.ShapeDtypeStruct((T, LANE), jnp.int32),
                   jax.ShapeDtypeStruct((1, LANE), F32)],
        scratch_shapes=[pltpu.VMEM((8, LANE), F32), pltpu.VMEM((8, LANE), F32)],
        compiler_params=_cparams(("arbitrary", "arbitrary")),
        name="route",
    )(logits, tril)


def _routing(logits, n_tokens, *, tm):
    gate, dest, cnt = _route_call(logits, tm=tm)
    gate = gate[:, :TOP_K]
    dest = dest[:, :TOP_K]
    counts = cnt[0, :N_EXPERTS].astype(jnp.int32)
    padded = ((counts + MOE_BLOCK - 1) // MOE_BLOCK) * MOE_BLOCK
    cum_padded = jnp.cumsum(padded)
    starts_padded = cum_padded - padded
    starts_sorted = jnp.cumsum(counts) - counts
    n_blocks = n_tokens * TOP_K // MOE_BLOCK + N_EXPERTS
    block_start = jnp.arange(n_blocks, dtype=jnp.int32) * MOE_BLOCK
    block_expert = jnp.minimum(jnp.sum((block_start[:, None] >= cum_padded[None, :]).astype(jnp.int32), axis=1),
                               N_EXPERTS - 1)
    n_active = (cum_padded[-1] // MOE_BLOCK).astype(jnp.int32).reshape(1)
    tok = jnp.broadcast_to(jnp.arange(n_tokens, dtype=jnp.int32)[:, None], dest.shape)
    _, tok_sorted = lax.sort_key_val(dest.reshape(-1), tok.reshape(-1))
    per_expert = jnp.stack([starts_padded - starts_sorted, starts_padded, counts], axis=1)
    per_block = _gather_rows(per_expert, block_expert)
    offs = jnp.arange(MOE_BLOCK, dtype=jnp.int32)[None, :]
    slot = block_start[:, None] + offs
    src = jnp.clip(slot - per_block[:, 0:1], 0, n_tokens * TOP_K - 1)
    valid = (slot - per_block[:, 1:2]) < per_block[:, 2:3]
    slot_token = jnp.where(valid, _gather_rows(tok_sorted, src), slot % n_tokens).reshape(-1)
    return gate, dest, slot_token, block_expert, n_active


def _gather_rows(a, idx):
    return a.at[idx].get(mode="promise_in_bounds")


N_STREAMS = 1


def _layer(x2, prep, w_gate_up, bgu, w_down, bd, tables, *, layer, batch, seq, alpha):
    T, D = x2.shape
    tm = min(512, seq)
    tq = min(512, seq)
    tk = min(1024, seq // 2)
    cs, mq_tab, mk_tab = tables
    naq, nak, nav, gqo, gko, gvo, mqo, mko, mvo = _qkv_call(
        x2, prep["wqkv"], cs, mq_tab, mk_tab, prep["gq"], prep["gk"], prep["nq"], prep["nkv"],
        prep["wuq"], prep["wuk"], prep["wuv"], seq=seq, tm=tm)
    oa = _na_call(naq, nak, nav, prep["na_bias"], batch=batch, seq=seq)
    ob = _flash_call(gqo, gko, gvo, batch=batch, seq=seq, kv_heads=GQA_KV_HEADS,
                     units=tuple((0, r * LANE) for r in range(GQA_REP)),
                     tu=tq, tk=tk, sum_lane=HEAD_DIM, name="gqa_attn")
    oc = _flash_call(mqo, mko, mvo, batch=batch, seq=seq, kv_heads=MLA_HEADS,
                     units=((0, 0), (tq, 0)), tu=tq, tk=tk, sum_lane=MLA_V, name="mla_attn")
    x1, x1b, logits = _merge_call(
        x2, oa, ob, oc, prep["wg"], prep["wa"], prep["wb"], prep["wc"], prep["wo"],
        prep["ln1_g"], prep["ln1_b"], prep["wrh"], prep["wrl"], prep["br"], alpha=alpha, tm=tm)
    gate, dest, slot_token, block_expert, n_active = _routing(logits, T, tm=tm)
    xs = _gather_rows(x1b, slot_token)
    y = _moe_call(block_expert, n_active, xs, w_gate_up, bgu, w_down, bd, layer=layer)
    yk = _gather_rows(y, dest.T.reshape(-1)).reshape(TOP_K, T, D)
    return _final_call(x1, yk, gate, prep["ln2_g"], prep["ln2_b"], alpha=alpha, tm=tm)


def kernel(x, w_in, na_rpb, gqa_q_norm, gqa_k_norm, mla_q_norm, mla_kv_norm, w_uq, w_ukv, w_branch_a, w_branch_b, w_branch_c, w_out, ln1_g, ln1_b, w_router, b_router, w_gate_up, b_gate_up, w_down, b_down, ln2_g, ln2_b):
    B, S, D = x.shape
    depth = w_in.shape[0]
    alpha = (2.0 * depth) ** 0.25
    tables = _axial_tables(S)
    preps = []
    for l in range(depth):
        wqkv, gq, gk, wuq, wuk, wuv, wg, wb, wc = _layer_weights(
            w_in[l], gqa_q_norm[l], gqa_k_norm[l], w_uq[l], w_ukv[l], w_branch_b[l], w_branch_c[l])
        wr = jnp.pad(w_router[l], ((0, 0), (0, LANE - N_EXPERTS)))
        wrh = wr.astype(BF16)
        preps.append(dict(
            wqkv=wqkv, gq=gq, gk=gk, wuq=wuq, wuk=wuk, wuv=wuv, wg=wg, wb=wb, wc=wc,
            nq=mla_q_norm[l].reshape(1, -1), nkv=mla_kv_norm[l].reshape(1, -1),
            na_bias=_na_bias_table(na_rpb[l]), wa=w_branch_a[l].astype(BF16), wo=w_out[l].astype(BF16),
            ln1_g=ln1_g[l].reshape(1, D), ln1_b=ln1_b[l].reshape(1, D),
            ln2_g=ln2_g[l].reshape(1, D), ln2_b=ln2_b[l].reshape(1, D),
            wrh=wrh, wrl=jnp.concatenate([wrh, (wr - wrh.astype(F32)).astype(BF16)], axis=1),
            br=jnp.pad(b_router[l], (0, LANE - N_EXPERTS), constant_values=NEG_INF).reshape(1, LANE)))
    bgu = b_gate_up.reshape(depth, N_EXPERTS, 1, -1)
    bd = b_down.reshape(depth, N_EXPERTS, 1, -1)
    n_streams = N_STREAMS if B % N_STREAMS == 0 else 1
    bs = B // n_streams
    outs = []
    for g in range(n_streams):
        x2 = x[g * bs:(g + 1) * bs].reshape(bs * S, D)
        for l in range(depth):
            x2 = _layer(x2, preps[l], w_gate_up, bgu, w_down, bd, tables,
                        layer=l, batch=bs, seq=S, alpha=alpha)
        outs.append(x2.reshape(bs, S, D))
    return outs[0] if n_streams == 1 else jnp.concatenate(outs, axis=0)
```

```python
import functools
import math

import jax
import jax.numpy as jnp
from jax import lax
from jax.experimental import pallas as pl
from jax.experimental.pallas import tpu as pltpu

F32 = jnp.float32
BF16 = jnp.bfloat16

LANE = 128
GRID_W = 64
HEAD_DIM = 64
NA_HEADS = 6
NA_WIN_H = 8
NA_WIN_W = 16
NA_W = NA_HEADS * HEAD_DIM
GQA_HEADS = 6
GQA_KV_HEADS = 2
GQA_REP = GQA_HEADS // GQA_KV_HEADS
MLA_HEADS = 4
MLA_Q_RANK = 384
MLA_KV_RANK = 256
MLA_NOPE = 64
MLA_ROPE = 32
MLA_V = 64
MLA_QK = MLA_NOPE + MLA_ROPE
ROPE_THETA = 10000.0
N_BRANCH = 3
N_EXPERTS = 32
TOP_K = 4
D_EXPERT = 1024
SWIGLU_LIMIT = 7.0
SWIGLU_ALPHA = 1.702
MOE_BLOCK = 512
LN_EPS = 1e-5
RMS_EPS = 1e-6
NEG_INF = -1e30
LOG2_E = math.log2(math.e)
VMEM_LIMIT = 56 * 1024 * 1024
TOKEN_TILE = 512
QUERY_TILE = 512
KEY_CHUNK = 1024
MLA_STACKED_ROWS = 2048

_C_NAQ = 0
_C_NAK = _C_NAQ + NA_W
_C_NAV = _C_NAK + NA_W
_C_GQ = _C_NAV + NA_W
_C_GK = _C_GQ + GQA_HEADS * LANE
_C_GV = _C_GK + GQA_KV_HEADS * LANE
_C_CQ = _C_GV + GQA_KV_HEADS * LANE
_C_CKV = _C_CQ + MLA_Q_RANK
_C_KR = _C_CKV + MLA_KV_RANK
_C_END = _C_KR + LANE


def _cparams(sem):
    return pltpu.CompilerParams(dimension_semantics=sem, vmem_limit_bytes=VMEM_LIMIT)


def _dot(a, b):
    return jnp.dot(a, b, preferred_element_type=F32)


def _dot_nt(a, b):
    return lax.dot_general(a, b, (((1,), (1,)), ((), ())), preferred_element_type=F32)


def _qkv_kernel(x_ref, w_ref, cs_ref, mq_ref, mk_ref, gq_ref, gk_ref, nq_ref, nkv_ref,
                wuq_ref, wuk_ref, wuv_ref,
                naq_ref, nak_ref, nav_ref, gqo_ref, gko_ref, gvo_ref, mqo_ref, mko_ref, mvo_ref, h_scr):
    tm = x_ref.shape[0]
    h_scr[...] = _dot(x_ref[...].astype(BF16), w_ref[...])

    def proj(c0, width):
        return h_scr[:, c0:c0 + width]

    naq_ref[...] = proj(_C_NAQ, NA_W).astype(BF16)
    nak_ref[...] = proj(_C_NAK, NA_W).astype(BF16)
    nav_ref[...] = proj(_C_NAV, NA_W).astype(BF16)

    lane = lax.broadcasted_iota(jnp.int32, (tm, LANE), 1)
    cs = cs_ref[...]

    def norm_rope(hc, gain):
        r = lax.rsqrt(jnp.mean(hc * hc, axis=-1, keepdims=True) + RMS_EPS)
        a = hc * r * (gain * cs)
        return jnp.where(lane < HEAD_DIM, a + pltpu.roll(a, HEAD_DIM, 1), 0.0)

    gq_gain = gq_ref[...]
    for h in range(GQA_HEADS):
        hc = proj(_C_GQ + h * LANE, LANE)
        gqo_ref[:, h * LANE:(h + 1) * LANE] = norm_rope(hc, gq_gain).astype(BF16)
    gk_gain = gk_ref[...]
    for g in range(GQA_KV_HEADS):
        hc = proj(_C_GK + g * LANE, LANE)
        gko_ref[:, g * LANE:(g + 1) * LANE] = norm_rope(hc, gk_gain).astype(BF16)
        hv = proj(_C_GV + g * LANE, LANE)
        gvo_ref[:, g * LANE:(g + 1) * LANE] = jnp.where(lane == HEAD_DIM, 1.0, hv).astype(BF16)

    def rms(v, gain):
        r = lax.rsqrt(jnp.mean(v * v, axis=-1, keepdims=True) + RMS_EPS)
        return (v * r * gain).astype(BF16)

    def mla_rope(b):
        summed = b + pltpu.roll(b, LANE - MLA_ROPE, 1)
        return jnp.where(lane < MLA_NOPE, b, jnp.where(lane < MLA_QK, summed, 0.0))

    hq = _dot(rms(proj(_C_CQ, MLA_Q_RANK), nq_ref[...]), wuq_ref[...])
    mq = mq_ref[...]
    for h in range(MLA_HEADS):
        mqo_ref[:, h * LANE:(h + 1) * LANE] = mla_rope(hq[:, h * LANE:(h + 1) * LANE] * mq).astype(BF16)

    ckv = rms(proj(_C_CKV, MLA_KV_RANK), nkv_ref[...])
    hk = _dot(ckv, wuk_ref[...])
    hv = _dot(ckv, wuv_ref[...])
    kr = mla_rope(proj(_C_KR, LANE) * mk_ref[...])
    for h in range(MLA_HEADS):
        sl = slice(h * LANE, (h + 1) * LANE)
        mko_ref[:, sl] = (hk[:, sl] + kr).astype(BF16)
        mvo_ref[:, sl] = jnp.where(lane == MLA_V, 1.0, hv[:, sl]).astype(BF16)


def _qkv_call(x2, wqkv, cs, mq, mk, gq, gk, nq, nkv, wuq, wuk, wuv, *, seq, tm):
    T, D = x2.shape
    ns = seq // tm

    def rows(width):
        return pl.BlockSpec((tm, width), lambda i: (i, 0))

    def pos(width):
        return pl.BlockSpec((tm, width), lambda i: (i % ns, 0))

    def whole(a):
        return pl.BlockSpec(a.shape, lambda i: (0,) * a.ndim)

    widths = [NA_W, NA_W, NA_W, GQA_HEADS * LANE, GQA_KV_HEADS * LANE, GQA_KV_HEADS * LANE,
              MLA_HEADS * LANE, MLA_HEADS * LANE, MLA_HEADS * LANE]
    return pl.pallas_call(
        _qkv_kernel,
        grid=(T // tm,),
        in_specs=[rows(D), whole(wqkv), pos(LANE), pos(LANE), pos(LANE), whole(gq), whole(gk),
                  whole(nq), whole(nkv), whole(wuq), whole(wuk), whole(wuv)],
        out_specs=[rows(w) for w in widths],
        out_shape=[jax.ShapeDtypeStruct((T, w), BF16) for w in widths],
        scratch_shapes=[pltpu.VMEM((tm, _C_END), F32)],
        compiler_params=_cparams(("parallel",)),
        name="qkv_proj",
    )(x2, wqkv, cs, mq, mk, gq, gk, nq, nkv, wuq, wuk, wuv)


NA_ROWS_PER_STEP = 8
NA_ROWS_PER_ITER = 4


def _na_kernel(q_ref, k_ref, v_ref, bias_ref, o_ref, *, n_rows):
    j = pl.program_id(1)
    win = NA_WIN_H * GRID_W
    lane = lax.broadcasted_iota(jnp.int32, (GRID_W, LANE), 1)
    low = lane < HEAD_DIM

    def rows_body(a2, carry):
        work = []
        for rr in range(NA_ROWS_PER_ITER):
            a = a2 * NA_ROWS_PER_ITER + rr
            r = j * NA_ROWS_PER_STEP + a
            r0 = jnp.clip(r - NA_WIN_H // 2, 0, n_rows - NA_WIN_H)
            variant = r - r0
            qrow = pl.ds(pl.multiple_of(a * GRID_W, GRID_W), GRID_W)
            krow = pl.ds(pl.multiple_of(r0 * GRID_W, GRID_W), win)
            for pair in range(NA_HEADS // 2):
                cols = slice(pair * LANE, (pair + 1) * LANE)
                qp = q_ref[qrow, cols]
                kp = k_ref[krow, cols]
                for half in range(2):
                    qm = jnp.where(low if half == 0 else jnp.logical_not(low), qp, jnp.zeros_like(qp))
                    s = _dot_nt(qm, kp) + bias_ref[variant, 2 * pair + half]
                    work.append((qrow, krow, cols, half, s))
        probs = []
        for qrow, krow, cols, half, s in work:
            m = jnp.max(s, axis=-1, keepdims=True)
            p = jnp.exp(s - m)
            probs.append((p.astype(BF16), jnp.sum(p, axis=-1, keepdims=True)))
        outs = []
        for (qrow, krow, cols, half, _), (p, l) in zip(work, probs):
            outs.append(_dot(p, v_ref[krow, cols]) / l)
        for n in range(0, len(work), 2):
            qrow, _, cols, _, _ = work[n]
            o_ref[qrow, cols] = jnp.where(low, outs[n], outs[n + 1]).astype(BF16)
        return carry

    lax.fori_loop(0, NA_ROWS_PER_STEP // NA_ROWS_PER_ITER, rows_body, 0)


def _na_call(q, k, v, bias, *, batch, seq):
    n_rows = seq // GRID_W
    steps = n_rows // NA_ROWS_PER_STEP
    tq = NA_ROWS_PER_STEP * GRID_W
    return pl.pallas_call(
        functools.partial(_na_kernel, n_rows=n_rows),
        grid=(batch, steps),
        in_specs=[pl.BlockSpec((tq, NA_W), lambda b, j: (b * steps + j, 0)),
                  pl.BlockSpec((seq, NA_W), lambda b, j: (b, 0)),
                  pl.BlockSpec((seq, NA_W), lambda b, j: (b, 0)),
                  pl.BlockSpec(bias.shape, lambda b, j: (0, 0, 0, 0))],
        out_specs=pl.BlockSpec((tq, NA_W), lambda b, j: (b * steps + j, 0)),
        out_shape=jax.ShapeDtypeStruct(q.shape, BF16),
        compiler_params=_cparams(("parallel", "arbitrary")),
        name="na_attn",
    )(q, k, v, bias)


def _na_bias_table(rpb):
    cols = jnp.arange(GRID_W)
    c0 = jnp.clip(cols - NA_WIN_W // 2, 0, GRID_W - NA_WIN_W)
    in_win = (cols[None, :] >= c0[:, None]) & (cols[None, :] < c0[:, None] + NA_WIN_W)
    idx_c = jnp.clip(cols[None, :] - cols[:, None] + (NA_WIN_W - 1), 0, 2 * NA_WIN_W - 2)
    variant = jnp.arange(NA_WIN_H)
    idx_r = jnp.arange(NA_WIN_H)[None, :] - variant[:, None] + (NA_WIN_H - 1)
    b = rpb.astype(F32)[:, idx_r]
    b = b[..., idx_c]
    b = jnp.where(in_win[None, None, None], b, NEG_INF)
    b = b.transpose(1, 0, 3, 2, 4)
    return b.reshape(NA_WIN_H, NA_HEADS, GRID_W, NA_WIN_H * GRID_W)


def _flash_kernel(q_ref, k_ref, v_ref, o_ref, q_scr, s0, s1, p0, p1, a0, a1, m_scr, acc_scr,
                  *, units, tu, tk, nk, sum_lane):
    s_slot, p_slot, a_slot = (s0, s1), (p0, p1), (a0, a1)
    n_units = len(units)
    for u, (r0, c0) in enumerate(units):
        q_scr[u * tu:(u + 1) * tu, :] = q_ref[r0:r0 + tu, c0:c0 + LANE]
    m_scr[...] = jnp.full(m_scr.shape, NEG_INF, F32)
    acc_scr[...] = jnp.zeros(acc_scr.shape, F32)
    whole = (slice(0, n_units * tu),)
    per_unit = tuple(slice(u * tu, (u + 1) * tu) for u in range(n_units))

    def chunk(ref, c):
        start = c * tk if isinstance(c, int) else pl.multiple_of(c * tk, tk)
        return ref[pl.ds(start, tk), :]

    def scores(c, slot, parts):
        k = chunk(k_ref, c)
        for r in parts:
            s_slot[slot][r, :] = _dot_nt(q_scr[r, :], k)

    def softmax(slot, parts):
        for r in parts:
            s = s_slot[slot][r, :]
            m_prev = m_scr[r, :]
            m_new = jnp.maximum(m_prev, jnp.max(s, axis=-1, keepdims=True))
            a_slot[slot][r, :] = jnp.exp2(m_prev - m_new)
            p_slot[slot][r, :] = jnp.exp2(s - m_new[:, :1]).astype(BF16)
            m_scr[r, :] = m_new

    def accumulate(c, slot, parts):
        v = chunk(v_ref, c)
        for r in parts:
            acc_scr[r, :] = a_slot[slot][r, :] * acc_scr[r, :] + _dot(p_slot[slot][r, :], v)

    scores(0, 0, per_unit)
    scores(1, 1, per_unit)
    softmax(0, per_unit)

    def body(j, carry):
        c1 = 2 * j + 1
        scores(c1 + 1, 0, whole)
        accumulate(c1 - 1, 0, whole)
        softmax(1, whole)
        scores(c1 + 2, 1, whole)
        accumulate(c1, 1, whole)
        softmax(0, whole)
        return carry

    lax.fori_loop(0, (nk - 2) // 2, body, 0)
    accumulate(nk - 2, 0, per_unit)
    softmax(1, per_unit)
    accumulate(nk - 1, 1, per_unit)
    for u, (r0, c0) in enumerate(units):
        acc = acc_scr[u * tu:(u + 1) * tu, :]
        o_ref[r0:r0 + tu, c0:c0 + LANE] = (acc / acc[:, sum_lane:sum_lane + 1]).astype(BF16)


def _flash_call(q, k, v, *, batch, seq, kv_heads, units, tu, tk, sum_lane, name):
    rows = max(r0 for r0, _ in units) + tu
    width = max(c0 for _, c0 in units) + LANE
    nq = seq // rows
    nk = seq // tk
    assert nk >= 2 and nk % 2 == 0
    stacked = len(units) * tu
    return pl.pallas_call(
        functools.partial(_flash_kernel, units=units, tu=tu, tk=tk, nk=nk, sum_lane=sum_lane),
        grid=(batch, kv_heads, nq),
        in_specs=[pl.BlockSpec((rows, width), lambda b, g, i: (b * nq + i, g)),
                  pl.BlockSpec((seq, LANE), lambda b, g, i: (b, g)),
                  pl.BlockSpec((seq, LANE), lambda b, g, i: (b, g))],
        out_specs=pl.BlockSpec((rows, width), lambda b, g, i: (b * nq + i, g)),
        out_shape=jax.ShapeDtypeStruct(q.shape, BF16),
        scratch_shapes=[pltpu.VMEM((stacked, LANE), BF16),
                        pltpu.VMEM((stacked, tk), F32), pltpu.VMEM((stacked, tk), F32),
                        pltpu.VMEM((stacked, tk), BF16), pltpu.VMEM((stacked, tk), BF16),
                        pltpu.VMEM((stacked, LANE), F32), pltpu.VMEM((stacked, LANE), F32),
                        pltpu.VMEM((stacked, LANE), F32), pltpu.VMEM((stacked, LANE), F32)],
        compiler_params=_cparams(("parallel", "parallel", "arbitrary")),
        name=name,
    )(q, k, v)


def _layer_norm(z, g, b):
    mu = jnp.mean(z, axis=-1, keepdims=True)
    zc = z - mu
    var = jnp.mean(zc * zc, axis=-1, keepdims=True)
    return zc * lax.rsqrt(var + LN_EPS) * g + b


def _merge_kernel(x_ref, oa_ref, ob_ref, oc_ref, wg_ref, wa_ref, wb_ref, wc_ref, wo_ref,
                  lng_ref, lnb_ref, wrh_ref, wrl_ref, br_ref, x1_ref, x1b_ref, logit_ref, *, alpha):
    d = x_ref.shape[1]
    x = x_ref[...]
    xb = x.astype(BF16)
    mixed = None
    for i, (o_ref, w_ref) in enumerate(((oa_ref, wa_ref), (ob_ref, wb_ref), (oc_ref, wc_ref))):
        gate = jax.nn.sigmoid(_dot(xb, wg_ref[:, i * d:(i + 1) * d]))
        term = gate * _dot(o_ref[...], w_ref[...])
        mixed = term if mixed is None else mixed + term
    z = alpha * x + _dot(mixed.astype(BF16), wo_ref[...])
    x1 = _layer_norm(z, lng_ref[...], lnb_ref[...])
    x1_ref[...] = x1
    hi = x1.astype(BF16)
    lo = (x1 - hi.astype(F32)).astype(BF16)
    x1b_ref[...] = hi
    both = _dot(hi, wrl_ref[...])
    logit_ref[...] = both[:, :LANE] + both[:, LANE:] + _dot(lo, wrh_ref[...]) + br_ref[...]


def _merge_call(x2, oa, ob, oc, wg, wa, wb, wc, wo, lng, lnb, wrh, wrl, br, *, alpha, tm):
    T, D = x2.shape

    def rows(width):
        return pl.BlockSpec((tm, width), lambda i: (i, 0))

    def whole(a):
        return pl.BlockSpec(a.shape, lambda i: (0,) * a.ndim)

    return pl.pallas_call(
        functools.partial(_merge_kernel, alpha=alpha),
        grid=(T // tm,),
        in_specs=[rows(D), rows(oa.shape[1]), rows(ob.shape[1]), rows(oc.shape[1]),
                  whole(wg), whole(wa), whole(wb), whole(wc), whole(wo),
                  whole(lng), whole(lnb), whole(wrh), whole(wrl), whole(br)],
        out_specs=[rows(D), rows(D), rows(LANE)],
        out_shape=[jax.ShapeDtypeStruct((T, D), F32), jax.ShapeDtypeStruct((T, D), BF16),
                   jax.ShapeDtypeStruct((T, LANE), F32)],
        compiler_params=_cparams(("parallel",)),
        name="merge_ln_router",
    )(x2, oa, ob, oc, wg, wa, wb, wc, wo, lng, lnb, wrh, wrl, br)


def _moe_kernel(be_ref, na_ref, xs_ref, wgu_ref, bgu_ref, wd_ref, bd_ref, y_ref, wgu_bf, wd_bf):
    i = pl.program_id(0)

    @pl.when(i >= na_ref[0])
    def _():
        y_ref[...] = jnp.zeros(y_ref.shape, y_ref.dtype)

    @pl.when(i < na_ref[0])
    def _():
        @pl.when(jnp.logical_or(i == 0, be_ref[i] != be_ref[jnp.maximum(i - 1, 0)]))
        def _():
            wgu_bf[...] = wgu_ref[0, 0].astype(BF16)
            wd_bf[...] = wd_ref[0, 0].astype(BF16)

        h = _dot(xs_ref[...], wgu_bf[...]) + bgu_ref[0, 0]
        g = jnp.minimum(h[:, :D_EXPERT], SWIGLU_LIMIT)
        u = jnp.clip(h[:, D_EXPERT:], -SWIGLU_LIMIT, SWIGLU_LIMIT)
        a = g * jax.nn.sigmoid(SWIGLU_ALPHA * g) * (u + 1.0)
        y_ref[...] = (_dot(a.astype(BF16), wd_bf[...]) + bd_ref[0, 0]).astype(y_ref.dtype)


def _moe_call(block_expert, n_active, xs, wgu, bgu, wd, bd, *, layer):
    P, D = xs.shape
    n_blocks = P // MOE_BLOCK

    def blk(i, be, na):
        return (jnp.minimum(i, na[0] - 1), 0)

    def per_expert(i, be, na):
        return (layer, be[jnp.minimum(i, na[0] - 1)], 0, 0)

    grid_spec = pltpu.PrefetchScalarGridSpec(
        num_scalar_prefetch=2,
        grid=(n_blocks,),
        in_specs=[pl.BlockSpec((MOE_BLOCK, D), blk),
                  pl.BlockSpec((1, 1, D, 2 * D_EXPERT), per_expert),
                  pl.BlockSpec((1, 1, 1, 2 * D_EXPERT), per_expert),
                  pl.BlockSpec((1, 1, D_EXPERT, D), per_expert),
                  pl.BlockSpec((1, 1, 1, D), per_expert)],
        out_specs=pl.BlockSpec((MOE_BLOCK, D), lambda i, be, na: (i, 0)),
        scratch_shapes=[pltpu.VMEM((D, 2 * D_EXPERT), BF16), pltpu.VMEM((D_EXPERT, D), BF16)],
    )
    return pl.pallas_call(
        _moe_kernel,
        grid_spec=grid_spec,
        out_shape=jax.ShapeDtypeStruct((P, D), BF16),
        compiler_params=_cparams(("arbitrary",)),
        name="moe_ffn",
    )(block_expert, n_active, xs, wgu, bgu, wd, bd)


def _final_kernel(x_ref, y_ref, gate_ref, lng_ref, lnb_ref, o_ref, *, alpha):
    gate = gate_ref[...]
    f = None
    for k in range(TOP_K):
        term = y_ref[k].astype(F32) * gate[:, k:k + 1]
        f = term if f is None else f + term
    o_ref[...] = _layer_norm(alpha * x_ref[...] + f, lng_ref[...], lnb_ref[...])


def _final_call(x1, yk, gate, lng, lnb, *, alpha, tm):
    T, D = x1.shape
    return pl.pallas_call(
        functools.partial(_final_kernel, alpha=alpha),
        grid=(T // tm,),
        in_specs=[pl.BlockSpec((tm, D), lambda i: (i, 0)),
                  pl.BlockSpec((TOP_K, tm, D), lambda i: (0, i, 0)),
                  pl.BlockSpec((tm, TOP_K), lambda i: (i, 0)),
                  pl.BlockSpec((1, D), lambda i: (0, 0)),
                  pl.BlockSpec((1, D), lambda i: (0, 0))],
        out_specs=pl.BlockSpec((tm, D), lambda i: (i, 0)),
        out_shape=jax.ShapeDtypeStruct((T, D), F32),
        compiler_params=_cparams(("parallel",)),
        name="combine_ln",
    )(x1, yk, gate, lng, lnb)


def _rot_half(w):
    half = w.shape[-1] // 2
    return jnp.concatenate([w[..., half:], w[..., :half]], axis=-1)


def _axial_tables(seq):
    def cos_sin(dim):
        quarter = dim // 4
        inv = ROPE_THETA ** (-jnp.arange(quarter, dtype=F32) / quarter)
        t = jnp.arange(seq)
        row = (t // GRID_W).astype(F32)
        col = (t % GRID_W).astype(F32)
        ang = jnp.concatenate([row[:, None] * inv, col[:, None] * inv], -1)
        return jnp.cos(ang), jnp.sin(ang)

    c64, s64 = cos_sin(HEAD_DIM)
    c32, s32 = cos_sin(MLA_ROPE)
    cs = jnp.concatenate([c64, c64, -s64, s64], -1)
    m = jnp.concatenate([jnp.ones((seq, MLA_NOPE), F32), c32, c32, -s32, s32], -1)
    return cs, m * (MLA_QK ** -0.5 * LOG2_E), m


def _layer_weights(w_in, gqa_q_norm, gqa_k_norm, w_uq, w_ukv, w_branch_b, w_branch_c):
    D = w_in.shape[0]
    widths = [NA_W, NA_W, NA_W, GQA_HEADS * HEAD_DIM, GQA_KV_HEADS * HEAD_DIM, GQA_KV_HEADS * HEAD_DIM,
              MLA_Q_RANK, MLA_KV_RANK, MLA_ROPE, N_BRANCH * D]
    offs = [0]
    for w in widths:
        offs.append(offs[-1] + w)
    na_q, na_k, na_v, g_q, g_k, g_v, c_q, c_kv, k_r, gates = [w_in[:, offs[i]:offs[i + 1]] for i in range(10)]

    def heads_with_rot(w, n):
        w = w.reshape(D, n, HEAD_DIM)
        return jnp.concatenate([w, _rot_half(w)], -1).reshape(D, n * LANE)

    def heads_padded(w, n, width):
        w = w.reshape(w.shape[0], n, width)
        return jnp.pad(w, ((0, 0), (0, 0), (0, LANE - width))).reshape(w.shape[0], n * LANE)

    kr_cols = jnp.concatenate([jnp.zeros((D, MLA_NOPE), F32), k_r, _rot_half(k_r)], -1)
    wqkv = jnp.concatenate([na_q * (HEAD_DIM ** -0.5), na_k, na_v,
                            heads_with_rot(g_q, GQA_HEADS), heads_with_rot(g_k, GQA_KV_HEADS),
                            heads_padded(g_v, GQA_KV_HEADS, HEAD_DIM), c_q, c_kv, kr_cols], -1).astype(BF16)

    gq = (jnp.concatenate([gqa_q_norm, _rot_half(gqa_q_norm)]) * (HEAD_DIM ** -0.5 * LOG2_E)).reshape(1, LANE)
    gk = jnp.concatenate([gqa_k_norm, _rot_half(gqa_k_norm)]).reshape(1, LANE)

    uq = w_uq.reshape(MLA_Q_RANK, MLA_HEADS, MLA_QK)
    uq_rope = uq[..., MLA_NOPE:]
    wuq = jnp.concatenate([uq, _rot_half(uq_rope)], -1).reshape(MLA_Q_RANK, MLA_HEADS * LANE).astype(BF16)
    ukv = w_ukv.reshape(MLA_KV_RANK, MLA_HEADS, MLA_NOPE + MLA_V)
    wuk = heads_padded(ukv[..., :MLA_NOPE].reshape(MLA_KV_RANK, -1), MLA_HEADS, MLA_NOPE).astype(BF16)
    wuv = heads_padded(ukv[..., MLA_NOPE:].reshape(MLA_KV_RANK, -1), MLA_HEADS, MLA_V).astype(BF16)

    def rows_padded(w, n, width):
        w = w.reshape(n, width, D)
        return jnp.pad(w, ((0, 0), (0, LANE - width), (0, 0))).reshape(n * LANE, D).astype(BF16)

    wb = rows_padded(w_branch_b, GQA_HEADS, HEAD_DIM)
    wc = rows_padded(w_branch_c, MLA_HEADS, MLA_V)
    return wqkv, gq, gk, wuq, wuk, wuv, gates.astype(BF16), wb, wc


def _route_kernel(logit_ref, tril_ref, gate_ref, dest_ref, cnt_ref, cnt_scr, run_scr):
    phase = pl.program_id(0)
    i = pl.program_id(1)
    tm = logit_ref.shape[0]
    lane = lax.broadcasted_iota(jnp.int32, (tm, LANE), 1).astype(F32)

    @pl.when(jnp.logical_and(phase == 0, i == 0))
    def _():
        cnt_scr[...] = jnp.zeros(cnt_scr.shape, F32)
        run_scr[...] = jnp.zeros(run_scr.shape, F32)

    logits = logit_ref[...]
    sel = jnp.zeros((tm, LANE), F32)
    vals, hits = [], []
    for _ in range(TOP_K):
        m = jnp.max(logits, axis=-1, keepdims=True)
        idx = jnp.min(jnp.where(logits == m, lane, float(LANE)), axis=-1, keepdims=True)
        hit = lane == idx
        sel = jnp.where(hit, 1.0, sel)
        logits = jnp.where(hit, -jnp.inf, logits)
        vals.append(m)
        hits.append(hit)
    tile_total = jnp.sum(sel, axis=0, keepdims=True)

    @pl.when(phase == 0)
    def _():
        cnt_scr[...] = cnt_scr[...] + tile_total

    @pl.when(phase == 1)
    def _():
        incl = _dot(tril_ref[...], sel.astype(BF16))
        counts = cnt_scr[...]
        padded = jnp.floor((counts + (MOE_BLOCK - 1)) * (1.0 / MOE_BLOCK)) * MOE_BLOCK
        lane8 = lax.broadcasted_iota(jnp.int32, counts.shape, 1)
        cum = padded
        shift = 1
        while shift < LANE:
            cum = cum + jnp.where(lane8 >= shift, pltpu.roll(cum, shift, 1), 0.0)
            shift *= 2
        starts = (cum - padded) + run_scr[...]
        base = starts[0:1, :] + (incl - sel)
        es = [jnp.exp(v - vals[0]) for v in vals]
        denom = es[0] + es[1] + es[2] + es[3]
        gate_out = jnp.zeros((tm, LANE), F32)
        dest_out = jnp.zeros((tm, LANE), F32)
        for k in range(TOP_K):
            d_k = jnp.sum(jnp.where(hits[k], base, 0.0), axis=-1, keepdims=True)
            gate_out = jnp.where(lane == float(k), es[k] / denom, gate_out)
            dest_out = jnp.where(lane == float(k), d_k, dest_out)
        gate_ref[...] = gate_out
        dest_ref[...] = dest_out.astype(jnp.int32)
        run_scr[...] = run_scr[...] + tile_total
        cnt_ref[...] = counts[0:1, :]


def _route_call(logits, *, tm):
    T = logits.shape[0]
    nt = T // tm
    tril = (jnp.arange(tm)[:, None] >= jnp.arange(tm)[None, :]).astype(BF16)
    return pl.pallas_call(
        _route_kernel,
        grid=(2, nt),
        in_specs=[pl.BlockSpec((tm, LANE), lambda p, i: (i, 0)),
                  pl.BlockSpec((tm, tm), lambda p, i: (0, 0))],
        out_specs=[pl.BlockSpec((tm, LANE), lambda p, i: (p * i, 0)),
                   pl.BlockSpec((tm, LANE), lambda p, i: (p * i, 0)),
                   pl.BlockSpec((1, LANE), lambda p, i: (0, 0))],
        out_shape=[jax.ShapeDtypeStruct((T, LANE), F32), jax.ShapeDtypeStruct((T, LANE), jnp.int32),
                   jax.ShapeDtypeStruct((1, LANE), F32)],
        scratch_shapes=[pltpu.VMEM((8, LANE), F32), pltpu.VMEM((8, LANE), F32)],
        compiler_params=_cparams(("arbitrary", "arbitrary")),
        name="route",
    )(logits, tril)


def _routing(logits, n_tokens, *, tm):
    gate, dest, cnt = _route_call(logits, tm=tm)
    gate = gate[:, :TOP_K]
    dest = dest[:, :TOP_K]
    counts = cnt[0, :N_EXPERTS].astype(jnp.int32)
    padded = ((counts + MOE_BLOCK - 1) // MOE_BLOCK) * MOE_BLOCK
    cum_padded = jnp.cumsum(padded)
    starts_padded = cum_padded - padded
    starts_sorted = jnp.cumsum(counts) - counts
    n_blocks = n_tokens * TOP_K // MOE_BLOCK + N_EXPERTS
    block_start = jnp.arange(n_blocks, dtype=jnp.int32) * MOE_BLOCK
    block_expert = jnp.minimum(jnp.sum((block_start[:, None] >= cum_padded[None, :]).astype(jnp.int32), axis=1),
                               N_EXPERTS - 1)
    n_active = (cum_padded[-1] // MOE_BLOCK).astype(jnp.int32).reshape(1)
    tok = jnp.broadcast_to(jnp.arange(n_tokens, dtype=jnp.int32)[:, None], dest.shape)
    _, tok_sorted = lax.sort_key_val(dest.reshape(-1), tok.reshape(-1))
    per_expert = jnp.stack([starts_padded - starts_sorted, starts_padded, counts], axis=1)
    per_block = _gather_rows(per_expert, block_expert)
    offs = jnp.arange(MOE_BLOCK, dtype=jnp.int32)[None, :]
    slot = block_start[:, None] + offs
    src = jnp.clip(slot - per_block[:, 0:1], 0, n_tokens * TOP_K - 1)
    valid = (slot - per_block[:, 1:2]) < per_block[:, 2:3]
    slot_token = jnp.where(valid, _gather_rows(tok_sorted, src), slot % n_tokens).reshape(-1)
    return gate, dest, slot_token, block_expert, n_active


def _gather_rows(a, idx):
    return a.at[idx].get(mode="promise_in_bounds")


N_STREAMS = 1


def _layer(x2, prep, w_gate_up, bgu, w_down, bd, tables, *, layer, batch, seq, alpha):
    T, D = x2.shape
    tm = min(TOKEN_TILE, seq)
    tq = min(QUERY_TILE, seq)
    tk = min(KEY_CHUNK, seq // 2)
    mla_units = max(1, min(MLA_STACKED_ROWS, seq) // tq)
    cs, mq_tab, mk_tab = tables
    naq, nak, nav, gqo, gko, gvo, mqo, mko, mvo = _qkv_call(
        x2, prep["wqkv"], cs, mq_tab, mk_tab, prep["gq"], prep["gk"], prep["nq"], prep["nkv"],
        prep["wuq"], prep["wuk"], prep["wuv"], seq=seq, tm=tm)
    oa = _na_call(naq, nak, nav, prep["na_bias"], batch=batch, seq=seq)
    ob = _flash_call(gqo, gko, gvo, batch=batch, seq=seq, kv_heads=GQA_KV_HEADS,
                     units=tuple((0, r * LANE) for r in range(GQA_REP)),
                     tu=tq, tk=tk, sum_lane=HEAD_DIM, name="gqa_attn")
    oc = _flash_call(mqo, mko, mvo, batch=batch, seq=seq, kv_heads=MLA_HEADS,
                     units=tuple((u * tq, 0) for u in range(mla_units)),
                     tu=tq, tk=tk, sum_lane=MLA_V, name="mla_attn")
    x1, x1b, logits = _merge_call(
        x2, oa, ob, oc, prep["wg"], prep["wa"], prep["wb"], prep["wc"], prep["wo"],
        prep["ln1_g"], prep["ln1_b"], prep["wrh"], prep["wrl"], prep["br"], alpha=alpha, tm=tm)
    gate, dest, slot_token, block_expert, n_active = _routing(logits, T, tm=tm)
    xs = _gather_rows(x1b, slot_token)
    y = _moe_call(block_expert, n_active, xs, w_gate_up, bgu, w_down, bd, layer=layer)
    yk = _gather_rows(y, dest.T.reshape(-1)).reshape(TOP_K, T, D)
    return _final_call(x1, yk, gate, prep["ln2_g"], prep["ln2_b"], alpha=alpha, tm=tm)


def kernel(x, w_in, na_rpb, gqa_q_norm, gqa_k_norm, mla_q_norm, mla_kv_norm, w_uq, w_ukv, w_branch_a, w_branch_b, w_branch_c, w_out, ln1_g, ln1_b, w_router, b_router, w_gate_up, b_gate_up, w_down, b_down, ln2_g, ln2_b):
    B, S, D = x.shape
    depth = w_in.shape[0]
    alpha = (2.0 * depth) ** 0.25
    tables = _axial_tables(S)
    preps = []
    for l in range(depth):
        wqkv, gq, gk, wuq, wuk, wuv, wg, wb, wc = _layer_weights(
            w_in[l], gqa_q_norm[l], gqa_k_norm[l], w_uq[l], w_ukv[l], w_branch_b[l], w_branch_c[l])
        wr = jnp.pad(w_router[l], ((0, 0), (0, LANE - N_EXPERTS)))
        wrh = wr.astype(BF16)
        preps.append(dict(
            wqkv=wqkv, gq=gq, gk=gk, wuq=wuq, wuk=wuk, wuv=wuv, wg=wg, wb=wb, wc=wc,
            nq=mla_q_norm[l].reshape(1, -1), nkv=mla_kv_norm[l].reshape(1, -1),
            na_bias=_na_bias_table(na_rpb[l]), wa=w_branch_a[l].astype(BF16), wo=w_out[l].astype(BF16),
            ln1_g=ln1_g[l].reshape(1, D), ln1_b=ln1_b[l].reshape(1, D),
            ln2_g=ln2_g[l].reshape(1, D), ln2_b=ln2_b[l].reshape(1, D),
            wrh=wrh, wrl=jnp.concatenate([wrh, (wr - wrh.astype(F32)).astype(BF16)], axis=1),
            br=jnp.pad(b_router[l], (0, LANE - N_EXPERTS), constant_values=NEG_INF).reshape(1, LANE)))
    bgu = b_gate_up.reshape(depth, N_EXPERTS, 1, -1)
    bd = b_down.reshape(depth, N_EXPERTS, 1, -1)
    n_streams = N_STREAMS if B % N_STREAMS == 0 else 1
    bs = B // n_streams
    outs = []
    for g in range(n_streams):
        x2 = x[g * bs:(g + 1) * bs].reshape(bs * S, D)
        for l in range(depth):
            x2 = _layer(x2, preps[l], w_gate_up, bgu, w_down, bd, tables,
                        layer=l, batch=bs, seq=S, alpha=alpha)
        outs.append(x2.reshape(bs, S, D))
    return outs[0] if n_streams == 1 else jnp.concatenate(outs, axis=0)
```

```python
import functools
import math

import jax
import jax.numpy as jnp
from jax import lax
from jax.experimental import pallas as pl
from jax.experimental.pallas import tpu as pltpu

F32 = jnp.float32
BF16 = jnp.bfloat16

LANE = 128
GRID_W = 64
HEAD_DIM = 64
NA_HEADS = 6
NA_WIN_H = 8
NA_WIN_W = 16
NA_W = NA_HEADS * HEAD_DIM
GQA_HEADS = 6
GQA_KV_HEADS = 2
GQA_REP = GQA_HEADS // GQA_KV_HEADS
MLA_HEADS = 4
MLA_Q_RANK = 384
MLA_KV_RANK = 256
MLA_NOPE = 64
MLA_ROPE = 32
MLA_V = 64
MLA_QK = MLA_NOPE + MLA_ROPE
ROPE_THETA = 10000.0
N_BRANCH = 3
N_EXPERTS = 32
TOP_K = 4
D_EXPERT = 1024
SWIGLU_LIMIT = 7.0
SWIGLU_ALPHA = 1.702
MOE_BLOCK = 512
LN_EPS = 1e-5
RMS_EPS = 1e-6
NEG_INF = -1e30
LOG2_E = math.log2(math.e)
VMEM_LIMIT = 56 * 1024 * 1024
TOKEN_TILE = 512
QUERY_TILE = 512
KEY_CHUNK = 1024
MLA_STACKED_ROWS = 2048

_C_NAQ = 0
_C_NAK = _C_NAQ + NA_W
_C_NAV = _C_NAK + NA_W
_C_GQ = _C_NAV + NA_W
_C_GK = _C_GQ + GQA_HEADS * LANE
_C_GV = _C_GK + GQA_KV_HEADS * LANE
_C_CQ = _C_GV + GQA_KV_HEADS * LANE
_C_CKV = _C_CQ + MLA_Q_RANK
_C_KR = _C_CKV + MLA_KV_RANK
_C_END = _C_KR + LANE


def _cparams(sem):
    return pltpu.CompilerParams(dimension_semantics=sem, vmem_limit_bytes=VMEM_LIMIT)


def _dot(a, b):
    return jnp.dot(a, b, preferred_element_type=F32)


def _dot_nt(a, b):
    return lax.dot_general(a, b, (((1,), (1,)), ((), ())), preferred_element_type=F32)


def _qkv_kernel(x_ref, w_ref, cs_ref, mq_ref, mk_ref, gq_ref, gk_ref, nq_ref, nkv_ref,
                wuq_ref, wuk_ref, wuv_ref,
                naq_ref, nak_ref, nav_ref, gqo_ref, gko_ref, gvo_ref, mqo_ref, mko_ref, mvo_ref, h_scr):
    tm = x_ref.shape[0]
    h_scr[...] = _dot(x_ref[...].astype(BF16), w_ref[...])

    def proj(c0, width):
        return h_scr[:, c0:c0 + width]

    naq_ref[...] = proj(_C_NAQ, NA_W).astype(BF16)
    nak_ref[...] = proj(_C_NAK, NA_W).astype(BF16)
    nav_ref[...] = proj(_C_NAV, NA_W).astype(BF16)

    lane = lax.broadcasted_iota(jnp.int32, (tm, LANE), 1)
    cs = cs_ref[...]

    def norm_rope(hc, gain):
        r = lax.rsqrt(jnp.mean(hc * hc, axis=-1, keepdims=True) + RMS_EPS)
        a = hc * r * (gain * cs)
        return jnp.where(lane < HEAD_DIM, a + pltpu.roll(a, HEAD_DIM, 1), 0.0)

    gq_gain = gq_ref[...]
    for h in range(GQA_HEADS):
        hc = proj(_C_GQ + h * LANE, LANE)
        gqo_ref[:, h * LANE:(h + 1) * LANE] = norm_rope(hc, gq_gain).astype(BF16)
    gk_gain = gk_ref[...]
    for g in range(GQA_KV_HEADS):
        hc = proj(_C_GK + g * LANE, LANE)
        gko_ref[:, g * LANE:(g + 1) * LANE] = norm_rope(hc, gk_gain).astype(BF16)
        hv = proj(_C_GV + g * LANE, LANE)
        gvo_ref[:, g * LANE:(g + 1) * LANE] = jnp.where(lane == HEAD_DIM, 1.0, hv).astype(BF16)

    def rms(v, gain):
        r = lax.rsqrt(jnp.mean(v * v, axis=-1, keepdims=True) + RMS_EPS)
        return (v * r * gain).astype(BF16)

    def mla_rope(b):
        summed = b + pltpu.roll(b, LANE - MLA_ROPE, 1)
        return jnp.where(lane < MLA_NOPE, b, jnp.where(lane < MLA_QK, summed, 0.0))

    hq = _dot(rms(proj(_C_CQ, MLA_Q_RANK), nq_ref[...]), wuq_ref[...])
    mq = mq_ref[...]
    for h in range(MLA_HEADS):
        mqo_ref[:, h * LANE:(h + 1) * LANE] = mla_rope(hq[:, h * LANE:(h + 1) * LANE] * mq).astype(BF16)

    ckv = rms(proj(_C_CKV, MLA_KV_RANK), nkv_ref[...])
    hk = _dot(ckv, wuk_ref[...])
    hv = _dot(ckv, wuv_ref[...])
    kr = mla_rope(proj(_C_KR, LANE) * mk_ref[...])
    for h in range(MLA_HEADS):
        sl = slice(h * LANE, (h + 1) * LANE)
        mko_ref[:, sl] = (hk[:, sl] + kr).astype(BF16)
        mvo_ref[:, sl] = jnp.where(lane == MLA_V, 1.0, hv[:, sl]).astype(BF16)


def _qkv_call(x2, wqkv, cs, mq, mk, gq, gk, nq, nkv, wuq, wuk, wuv, *, seq, tm):
    T, D = x2.shape
    ns = seq // tm

    def rows(width):
        return pl.BlockSpec((tm, width), lambda i: (i, 0))

    def pos(width):
        return pl.BlockSpec((tm, width), lambda i: (i % ns, 0))

    def whole(a):
        return pl.BlockSpec(a.shape, lambda i: (0,) * a.ndim)

    widths = [NA_W, NA_W, NA_W, GQA_HEADS * LANE, GQA_KV_HEADS * LANE, GQA_KV_HEADS * LANE,
              MLA_HEADS * LANE, MLA_HEADS * LANE, MLA_HEADS * LANE]
    return pl.pallas_call(
        _qkv_kernel,
        grid=(T // tm,),
        in_specs=[rows(D), whole(wqkv), pos(LANE), pos(LANE), pos(LANE), whole(gq), whole(gk),
                  whole(nq), whole(nkv), whole(wuq), whole(wuk), whole(wuv)],
        out_specs=[rows(w) for w in widths],
        out_shape=[jax.ShapeDtypeStruct((T, w), BF16) for w in widths],
        scratch_shapes=[pltpu.VMEM((tm, _C_END), F32)],
        compiler_params=_cparams(("parallel",)),
        name="qkv_proj",
    )(x2, wqkv, cs, mq, mk, gq, gk, nq, nkv, wuq, wuk, wuv)


NA_ROWS_PER_STEP = 8
NA_ROWS_PER_ITER = 4


def _na_kernel(q_ref, k_ref, v_ref, bias_ref, o_ref, *, n_rows):
    j = pl.program_id(1)
    win = NA_WIN_H * GRID_W
    lane = lax.broadcasted_iota(jnp.int32, (GRID_W, LANE), 1)
    low = lane < HEAD_DIM

    def rows_body(a2, carry):
        work = []
        for rr in range(NA_ROWS_PER_ITER):
            a = a2 * NA_ROWS_PER_ITER + rr
            r = j * NA_ROWS_PER_STEP + a
            r0 = jnp.clip(r - NA_WIN_H // 2, 0, n_rows - NA_WIN_H)
            variant = r - r0
            qrow = pl.ds(pl.multiple_of(a * GRID_W, GRID_W), GRID_W)
            krow = pl.ds(pl.multiple_of(r0 * GRID_W, GRID_W), win)
            for pair in range(NA_HEADS // 2):
                cols = slice(pair * LANE, (pair + 1) * LANE)
                qp = q_ref[qrow, cols]
                kp = k_ref[krow, cols]
                for half in range(2):
                    qm = jnp.where(low if half == 0 else jnp.logical_not(low), qp, jnp.zeros_like(qp))
                    s = _dot_nt(qm, kp) + bias_ref[variant, 2 * pair + half]
                    work.append((qrow, krow, cols, half, s))
        probs = []
        for qrow, krow, cols, half, s in work:
            m = jnp.max(s, axis=-1, keepdims=True)
            p = jnp.exp(s - m)
            probs.append((p.astype(BF16), jnp.sum(p, axis=-1, keepdims=True)))
        outs = []
        for (qrow, krow, cols, half, _), (p, l) in zip(work, probs):
            outs.append(_dot(p, v_ref[krow, cols]) / l)
        for n in range(0, len(work), 2):
            qrow, _, cols, _, _ = work[n]
            o_ref[qrow, cols] = jnp.where(low, outs[n], outs[n + 1]).astype(BF16)
        return carry

    lax.fori_loop(0, NA_ROWS_PER_STEP // NA_ROWS_PER_ITER, rows_body, 0)


def _na_call(q, k, v, bias, *, batch, seq):
    n_rows = seq // GRID_W
    steps = n_rows // NA_ROWS_PER_STEP
    tq = NA_ROWS_PER_STEP * GRID_W
    return pl.pallas_call(
        functools.partial(_na_kernel, n_rows=n_rows),
        grid=(batch, steps),
        in_specs=[pl.BlockSpec((tq, NA_W), lambda b, j: (b * steps + j, 0)),
                  pl.BlockSpec((seq, NA_W), lambda b, j: (b, 0)),
                  pl.BlockSpec((seq, NA_W), lambda b, j: (b, 0)),
                  pl.BlockSpec(bias.shape, lambda b, j: (0, 0, 0, 0))],
        out_specs=pl.BlockSpec((tq, NA_W), lambda b, j: (b * steps + j, 0)),
        out_shape=jax.ShapeDtypeStruct(q.shape, BF16),
        compiler_params=_cparams(("parallel", "arbitrary")),
        name="na_attn",
    )(q, k, v, bias)


def _na_bias_table(rpb):
    cols = jnp.arange(GRID_W)
    c0 = jnp.clip(cols - NA_WIN_W // 2, 0, GRID_W - NA_WIN_W)
    in_win = (cols[None, :] >= c0[:, None]) & (cols[None, :] < c0[:, None] + NA_WIN_W)
    idx_c = jnp.clip(cols[None, :] - cols[:, None] + (NA_WIN_W - 1), 0, 2 * NA_WIN_W - 2)
    variant = jnp.arange(NA_WIN_H)
    idx_r = jnp.arange(NA_WIN_H)[None, :] - variant[:, None] + (NA_WIN_H - 1)
    b = rpb.astype(F32)[:, idx_r]
    b = b[..., idx_c]
    b = jnp.where(in_win[None, None, None], b, NEG_INF)
    b = b.transpose(1, 0, 3, 2, 4)
    return b.reshape(NA_WIN_H, NA_HEADS, GRID_W, NA_WIN_H * GRID_W)


def _flash_kernel(q_ref, k_ref, v_ref, o_ref, q_scr, s0, s1, p0, p1, a0, a1, m_scr, acc_scr,
                  *, units, tu, tk, nk, sum_lane):
    s_slot, p_slot, a_slot = (s0, s1), (p0, p1), (a0, a1)
    n_units = len(units)
    for u, (r0, c0) in enumerate(units):
        q_scr[u * tu:(u + 1) * tu, :] = q_ref[r0:r0 + tu, c0:c0 + LANE]
    m_scr[...] = jnp.full(m_scr.shape, NEG_INF, F32)
    acc_scr[...] = jnp.zeros(acc_scr.shape, F32)
    whole = (slice(0, n_units * tu),)
    per_unit = tuple(slice(u * tu, (u + 1) * tu) for u in range(n_units))

    def chunk(ref, c):
        start = c * tk if isinstance(c, int) else pl.multiple_of(c * tk, tk)
        return ref[pl.ds(start, tk), :]

    def scores(c, slot, parts):
        k = chunk(k_ref, c)
        for r in parts:
            s_slot[slot][r, :] = _dot_nt(q_scr[r, :], k)

    def softmax(slot, parts):
        for r in parts:
            s = s_slot[slot][r, :]
            m_prev = m_scr[r, :]
            m_new = jnp.maximum(m_prev, jnp.max(s, axis=-1, keepdims=True))
            a_slot[slot][r, :] = jnp.exp2(m_prev - m_new)
            p_slot[slot][r, :] = jnp.exp2(s - m_new[:, :1]).astype(BF16)
            m_scr[r, :] = m_new

    def accumulate(c, slot, parts):
        v = chunk(v_ref, c)
        for r in parts:
            acc_scr[r, :] = a_slot[slot][r, :] * acc_scr[r, :] + _dot(p_slot[slot][r, :], v)

    scores(0, 0, per_unit)
    scores(1, 1, per_unit)
    softmax(0, per_unit)

    def body(j, carry):
        c1 = 2 * j + 1
        scores(c1 + 1, 0, whole)
        accumulate(c1 - 1, 0, whole)
        softmax(1, whole)
        scores(c1 + 2, 1, whole)
        accumulate(c1, 1, whole)
        softmax(0, whole)
        return carry

    lax.fori_loop(0, (nk - 2) // 2, body, 0)
    accumulate(nk - 2, 0, per_unit)
    softmax(1, per_unit)
    accumulate(nk - 1, 1, per_unit)
    for u, (r0, c0) in enumerate(units):
        acc = acc_scr[u * tu:(u + 1) * tu, :]
        o_ref[r0:r0 + tu, c0:c0 + LANE] = (acc / acc[:, sum_lane:sum_lane + 1]).astype(BF16)


def _flash_call(q, k, v, *, batch, seq, kv_heads, units, tu, tk, sum_lane, name):
    rows = max(r0 for r0, _ in units) + tu
    width = max(c0 for _, c0 in units) + LANE
    nq = seq // rows
    nk = seq // tk
    assert nk >= 2 and nk % 2 == 0
    stacked = len(units) * tu
    return pl.pallas_call(
        functools.partial(_flash_kernel, units=units, tu=tu, tk=tk, nk=nk, sum_lane=sum_lane),
        grid=(batch, kv_heads, nq),
        in_specs=[pl.BlockSpec((rows, width), lambda b, g, i: (b * nq + i, g)),
                  pl.BlockSpec((seq, LANE), lambda b, g, i: (b, g)),
                  pl.BlockSpec((seq, LANE), lambda b, g, i: (b, g))],
        out_specs=pl.BlockSpec((rows, width), lambda b, g, i: (b * nq + i, g)),
        out_shape=jax.ShapeDtypeStruct(q.shape, BF16),
        scratch_shapes=[pltpu.VMEM((stacked, LANE), BF16),
                        pltpu.VMEM((stacked, tk), F32), pltpu.VMEM((stacked, tk), F32),
                        pltpu.VMEM((stacked, tk), BF16), pltpu.VMEM((stacked, tk), BF16),
                        pltpu.VMEM((stacked, LANE), F32), pltpu.VMEM((stacked, LANE), F32),
                        pltpu.VMEM((stacked, LANE), F32), pltpu.VMEM((stacked, LANE), F32)],
        compiler_params=_cparams(("parallel", "parallel", "arbitrary")),
        name=name,
    )(q, k, v)


def _layer_norm(z, g, b):
    mu = jnp.mean(z, axis=-1, keepdims=True)
    zc = z - mu
    var = jnp.mean(zc * zc, axis=-1, keepdims=True)
    return zc * lax.rsqrt(var + LN_EPS) * g + b


def _merge_kernel(x_ref, oa_ref, ob_ref, oc_ref, wg_ref, wa_ref, wb_ref, wc_ref, wo_ref,
                  lng_ref, lnb_ref, wrh_ref, wrl_ref, br_ref, x1_ref, x1b_ref, logit_ref, *, alpha):
    d = x_ref.shape[1]
    x = x_ref[...]
    xb = x.astype(BF16)
    mixed = None
    for i, (o_ref, w_ref) in enumerate(((oa_ref, wa_ref), (ob_ref, wb_ref), (oc_ref, wc_ref))):
        gate = jax.nn.sigmoid(_dot(xb, wg_ref[:, i * d:(i + 1) * d]))
        term = gate * _dot(o_ref[...], w_ref[...])
        mixed = term if mixed is None else mixed + term
    z = alpha * x + _dot(mixed.astype(BF16), wo_ref[...])
    x1 = _layer_norm(z, lng_ref[...], lnb_ref[...])
    x1_ref[...] = x1
    hi = x1.astype(BF16)
    lo = (x1 - hi.astype(F32)).astype(BF16)
    x1b_ref[...] = hi
    both = _dot(hi, wrl_ref[...])
    logit_ref[...] = both[:, :LANE] + both[:, LANE:] + _dot(lo, wrh_ref[...]) + br_ref[...]


def _merge_call(x2, oa, ob, oc, wg, wa, wb, wc, wo, lng, lnb, wrh, wrl, br, *, alpha, tm):
    T, D = x2.shape

    def rows(width):
        return pl.BlockSpec((tm, width), lambda i: (i, 0))

    def whole(a):
        return pl.BlockSpec(a.shape, lambda i: (0,) * a.ndim)

    return pl.pallas_call(
        functools.partial(_merge_kernel, alpha=alpha),
        grid=(T // tm,),
        in_specs=[rows(D), rows(oa.shape[1]), rows(ob.shape[1]), rows(oc.shape[1]),
                  whole(wg), whole(wa), whole(wb), whole(wc), whole(wo),
                  whole(lng), whole(lnb), whole(wrh), whole(wrl), whole(br)],
        out_specs=[rows(D), rows(D), rows(LANE)],
        out_shape=[jax.ShapeDtypeStruct((T, D), F32), jax.ShapeDtypeStruct((T, D), BF16),
                   jax.ShapeDtypeStruct((T, LANE), F32)],
        compiler_params=_cparams(("parallel",)),
        name="merge_ln_router",
    )(x2, oa, ob, oc, wg, wa, wb, wc, wo, lng, lnb, wrh, wrl, br)


def _moe_kernel(be_ref, na_ref, xs_ref, wgu_ref, bgu_ref, wd_ref, bd_ref, y_ref, wgu_bf, wd_bf):
    i = pl.program_id(0)

    @pl.when(i >= na_ref[0])
    def _():
        y_ref[...] = jnp.zeros(y_ref.shape, y_ref.dtype)

    @pl.when(i < na_ref[0])
    def _():
        @pl.when(jnp.logical_or(i == 0, be_ref[i] != be_ref[jnp.maximum(i - 1, 0)]))
        def _():
            wgu_bf[...] = wgu_ref[0, 0].astype(BF16)
            wd_bf[...] = wd_ref[0, 0].astype(BF16)

        h = _dot(xs_ref[...], wgu_bf[...]) + bgu_ref[0, 0]
        g = jnp.minimum(h[:, :D_EXPERT], SWIGLU_LIMIT)
        u = jnp.clip(h[:, D_EXPERT:], -SWIGLU_LIMIT, SWIGLU_LIMIT)
        a = g * jax.nn.sigmoid(SWIGLU_ALPHA * g) * (u + 1.0)
        y_ref[...] = (_dot(a.astype(BF16), wd_bf[...]) + bd_ref[0, 0]).astype(y_ref.dtype)


def _moe_call(block_expert, n_active, xs, wgu, bgu, wd, bd, *, layer):
    P, D = xs.shape
    n_blocks = P // MOE_BLOCK

    def blk(i, be, na):
        return (jnp.minimum(i, na[0] - 1), 0)

    def per_expert(i, be, na):
        return (layer, be[jnp.minimum(i, na[0] - 1)], 0, 0)

    grid_spec = pltpu.PrefetchScalarGridSpec(
        num_scalar_prefetch=2,
        grid=(n_blocks,),
        in_specs=[pl.BlockSpec((MOE_BLOCK, D), blk),
                  pl.BlockSpec((1, 1, D, 2 * D_EXPERT), per_expert),
                  pl.BlockSpec((1, 1, 1, 2 * D_EXPERT), per_expert),
                  pl.BlockSpec((1, 1, D_EXPERT, D), per_expert),
                  pl.BlockSpec((1, 1, 1, D), per_expert)],
        out_specs=pl.BlockSpec((MOE_BLOCK, D), lambda i, be, na: (i, 0)),
        scratch_shapes=[pltpu.VMEM((D, 2 * D_EXPERT), BF16), pltpu.VMEM((D_EXPERT, D), BF16)],
    )
    return pl.pallas_call(
        _moe_kernel,
        grid_spec=grid_spec,
        out_shape=jax.ShapeDtypeStruct((P, D), BF16),
        compiler_params=_cparams(("arbitrary",)),
        name="moe_ffn",
    )(block_expert, n_active, xs, wgu, bgu, wd, bd)


def _final_kernel(x_ref, y_ref, gate_ref, lng_ref, lnb_ref, o_ref, *, alpha):
    gate = gate_ref[...]
    f = None
    for k in range(TOP_K):
        term = y_ref[k].astype(F32) * gate[:, k:k + 1]
        f = term if f is None else f + term
    o_ref[...] = _layer_norm(alpha * x_ref[...] + f, lng_ref[...], lnb_ref[...])


def _final_call(x1, yk, gate, lng, lnb, *, alpha, tm):
    T, D = x1.shape
    return pl.pallas_call(
        functools.partial(_final_kernel, alpha=alpha),
        grid=(T // tm,),
        in_specs=[pl.BlockSpec((tm, D), lambda i: (i, 0)),
                  pl.BlockSpec((TOP_K, tm, D), lambda i: (0, i, 0)),
                  pl.BlockSpec((tm, TOP_K), lambda i: (i, 0)),
                  pl.BlockSpec((1, D), lambda i: (0, 0)),
                  pl.BlockSpec((1, D), lambda i: (0, 0))],
        out_specs=pl.BlockSpec((tm, D), lambda i: (i, 0)),
        out_shape=jax.ShapeDtypeStruct((T, D), F32),
        compiler_params=_cparams(("parallel",)),
        name="combine_ln",
    )(x1, yk, gate, lng, lnb)


def _rot_half(w):
    half = w.shape[-1] // 2
    return jnp.concatenate([w[..., half:], w[..., :half]], axis=-1)


def _axial_tables(seq):
    def cos_sin(dim):
        quarter = dim // 4
        inv = ROPE_THETA ** (-jnp.arange(quarter, dtype=F32) / quarter)
        t = jnp.arange(seq)
        row = (t // GRID_W).astype(F32)
        col = (t % GRID_W).astype(F32)
        ang = jnp.concatenate([row[:, None] * inv, col[:, None] * inv], -1)
        return jnp.cos(ang), jnp.sin(ang)

    c64, s64 = cos_sin(HEAD_DIM)
    c32, s32 = cos_sin(MLA_ROPE)
    cs = jnp.concatenate([c64, c64, -s64, s64], -1)
    m = jnp.concatenate([jnp.ones((seq, MLA_NOPE), F32), c32, c32, -s32, s32], -1)
    return cs, m * (MLA_QK ** -0.5 * LOG2_E), m


def _layer_weights(w_in, gqa_q_norm, gqa_k_norm, w_uq, w_ukv, w_branch_b, w_branch_c):
    D = w_in.shape[0]
    widths = [NA_W, NA_W, NA_W, GQA_HEADS * HEAD_DIM, GQA_KV_HEADS * HEAD_DIM, GQA_KV_HEADS * HEAD_DIM,
              MLA_Q_RANK, MLA_KV_RANK, MLA_ROPE, N_BRANCH * D]
    offs = [0]
    for w in widths:
        offs.append(offs[-1] + w)
    na_q, na_k, na_v, g_q, g_k, g_v, c_q, c_kv, k_r, gates = [w_in[:, offs[i]:offs[i + 1]] for i in range(10)]

    def heads_with_rot(w, n):
        w = w.reshape(D, n, HEAD_DIM)
        return jnp.concatenate([w, _rot_half(w)], -1).reshape(D, n * LANE)

    def heads_padded(w, n, width):
        w = w.reshape(w.shape[0], n, width)
        return jnp.pad(w, ((0, 0), (0, 0), (0, LANE - width))).reshape(w.shape[0], n * LANE)

    kr_cols = jnp.concatenate([jnp.zeros((D, MLA_NOPE), F32), k_r, _rot_half(k_r)], -1)
    wqkv = jnp.concatenate([na_q * (HEAD_DIM ** -0.5), na_k, na_v,
                            heads_with_rot(g_q, GQA_HEADS), heads_with_rot(g_k, GQA_KV_HEADS),
                            heads_padded(g_v, GQA_KV_HEADS, HEAD_DIM), c_q, c_kv, kr_cols], -1).astype(BF16)

    gq = (jnp.concatenate([gqa_q_norm, _rot_half(gqa_q_norm)]) * (HEAD_DIM ** -0.5 * LOG2_E)).reshape(1, LANE)
    gk = jnp.concatenate([gqa_k_norm, _rot_half(gqa_k_norm)]).reshape(1, LANE)

    uq = w_uq.reshape(MLA_Q_RANK, MLA_HEADS, MLA_QK)
    uq_rope = uq[..., MLA_NOPE:]
    wuq = jnp.concatenate([uq, _rot_half(uq_rope)], -1).reshape(MLA_Q_RANK, MLA_HEADS * LANE).astype(BF16)
    ukv = w_ukv.reshape(MLA_KV_RANK, MLA_HEADS, MLA_NOPE + MLA_V)
    wuk = heads_padded(ukv[..., :MLA_NOPE].reshape(MLA_KV_RANK, -1), MLA_HEADS, MLA_NOPE).astype(BF16)
    wuv = heads_padded(ukv[..., MLA_NOPE:].reshape(MLA_KV_RANK, -1), MLA_HEADS, MLA_V).astype(BF16)

    def rows_padded(w, n, width):
        w = w.reshape(n, width, D)
        return jnp.pad(w, ((0, 0), (0, LANE - width), (0, 0))).reshape(n * LANE, D).astype(BF16)

    wb = rows_padded(w_branch_b, GQA_HEADS, HEAD_DIM)
    wc = rows_padded(w_branch_c, MLA_HEADS, MLA_V)
    return wqkv, gq, gk, wuq, wuk, wuv, gates.astype(BF16), wb, wc


def _route_kernel(logit_ref, tril_ref, gate_ref, dest_ref, cnt_ref, cnt_scr, run_scr):
    phase = pl.program_id(0)
    i = pl.program_id(1)
    tm = logit_ref.shape[0]
    lane = lax.broadcasted_iota(jnp.int32, (tm, LANE), 1).astype(F32)

    @pl.when(jnp.logical_and(phase == 0, i == 0))
    def _():
        cnt_scr[...] = jnp.zeros(cnt_scr.shape, F32)
        run_scr[...] = jnp.zeros(run_scr.shape, F32)

    logits = logit_ref[...]
    sel = jnp.zeros((tm, LANE), F32)
    vals, hits = [], []
    for _ in range(TOP_K):
        m = jnp.max(logits, axis=-1, keepdims=True)
        idx = jnp.min(jnp.where(logits == m, lane, float(LANE)), axis=-1, keepdims=True)
        hit = lane == idx
        sel = jnp.where(hit, 1.0, sel)
        logits = jnp.where(hit, -jnp.inf, logits)
        vals.append(m)
        hits.append(hit)
    tile_total = jnp.sum(sel, axis=0, keepdims=True)

    @pl.when(phase == 0)
    def _():
        cnt_scr[...] = cnt_scr[...] + tile_total

    @pl.when(phase == 1)
    def _():
        incl = _dot(tril_ref[...], sel.astype(BF16))
        counts = cnt_scr[...]
        padded = jnp.floor((counts + (MOE_BLOCK - 1)) * (1.0 / MOE_BLOCK)) * MOE_BLOCK
        lane8 = lax.broadcasted_iota(jnp.int32, counts.shape, 1)
        cum = padded
        shift = 1
        while shift < LANE:
            cum = cum + jnp.where(lane8 >= shift, pltpu.roll(cum, shift, 1), 0.0)
            shift *= 2
        starts = (cum - padded) + run_scr[...]
        base = starts[0:1, :] + (incl - sel)
        es = [jnp.exp(v - vals[0]) for v in vals]
        denom = es[0] + es[1] + es[2] + es[3]
        gate_out = jnp.zeros((tm, LANE), F32)
        dest_out = jnp.zeros((tm, LANE), F32)
        for k in range(TOP_K):
            d_k = jnp.sum(jnp.where(hits[k], base, 0.0), axis=-1, keepdims=True)
            gate_out = jnp.where(lane == float(k), es[k] / denom, gate_out)
            dest_out = jnp.where(lane == float(k), d_k, dest_out)
        gate_ref[...] = gate_out
        dest_ref[...] = dest_out.astype(jnp.int32)
        run_scr[...] = run_scr[...] + tile_total
        cnt_ref[...] = counts[0:1, :]


def _route_call(logits, *, tm):
    T = logits.shape[0]
    nt = T // tm
    tril = (jnp.arange(tm)[:, None] >= jnp.arange(tm)[None, :]).astype(BF16)
    return pl.pallas_call(
        _route_kernel,
        grid=(2, nt),
        in_specs=[pl.BlockSpec((tm, LANE), lambda p, i: (i, 0)),
                  pl.BlockSpec((tm, tm), lambda p, i: (0, 0))],
        out_specs=[pl.BlockSpec((tm, LANE), lambda p, i: (p * i, 0)),
                   pl.BlockSpec((tm, LANE), lambda p, i: (p * i, 0)),
                   pl.BlockSpec((1, LANE), lambda p, i: (0, 0))],
        out_shape=[jax.ShapeDtypeStruct((T, LANE), F32), jax.ShapeDtypeStruct((T, LANE), jnp.int32),
                   jax.ShapeDtypeStruct((1, LANE), F32)],
        scratch_shapes=[pltpu.VMEM((8, LANE), F32), pltpu.VMEM((8, LANE), F32)],
        compiler_params=_cparams(("arbitrary", "arbitrary")),
        name="route",
    )(logits, tril)


def _routing(logits, n_tokens, *, tm):
    gate, dest, cnt = _route_call(logits, tm=tm)
    gate = gate[:, :TOP_K]
    dest = dest[:, :TOP_K]
    counts = cnt[0, :N_EXPERTS].astype(jnp.int32)
    padded = ((counts + MOE_BLOCK - 1) // MOE_BLOCK) * MOE_BLOCK
    cum_padded = jnp.cumsum(padded)
    starts_padded = cum_padded - padded
    starts_sorted = jnp.cumsum(counts) - counts
    n_blocks = n_tokens * TOP_K // MOE_BLOCK + N_EXPERTS
    block_start = jnp.arange(n_blocks, dtype=jnp.int32) * MOE_BLOCK
    block_expert = jnp.minimum(jnp.sum((block_start[:, None] >= cum_padded[None, :]).astype(jnp.int32), axis=1),
                               N_EXPERTS - 1)
    n_active = (cum_padded[-1] // MOE_BLOCK).astype(jnp.int32).reshape(1)
    tok = jnp.broadcast_to(jnp.arange(n_tokens, dtype=jnp.int32)[:, None], dest.shape)
    _, tok_sorted = lax.sort_key_val(dest.reshape(-1), tok.reshape(-1))
    per_expert = jnp.stack([starts_padded - starts_sorted, starts_padded, counts], axis=1)
    per_block = _gather_rows(per_expert, block_expert)
    offs = jnp.arange(MOE_BLOCK, dtype=jnp.int32)[None, :]
    slot = block_start[:, None] + offs
    src = jnp.clip(slot - per_block[:, 0:1], 0, n_tokens * TOP_K - 1)
    valid = (slot - per_block[:, 1:2]) < per_block[:, 2:3]
    slot_token = jnp.where(valid, _gather_rows(tok_sorted, src), slot % n_tokens).reshape(-1)
    return gate, dest, slot_token, block_expert, n_active


def _gather_rows(a, idx):
    return a.at[idx].get(mode="promise_in_bounds")


def _layer(x2, prep, w_gate_up, bgu, w_down, bd, tables, *, layer, batch, seq, alpha):
    T, D = x2.shape
    tm = min(TOKEN_TILE, seq)
    tq = min(QUERY_TILE, seq)
    tk = min(KEY_CHUNK, seq // 2)
    mla_units = max(1, min(MLA_STACKED_ROWS, seq) // tq)
    cs, mq_tab, mk_tab = tables
    naq, nak, nav, gqo, gko, gvo, mqo, mko, mvo = _qkv_call(
        x2, prep["wqkv"], cs, mq_tab, mk_tab, prep["gq"], prep["gk"], prep["nq"], prep["nkv"],
        prep["wuq"], prep["wuk"], prep["wuv"], seq=seq, tm=tm)
    oa = _na_call(naq, nak, nav, prep["na_bias"], batch=batch, seq=seq)
    ob = _flash_call(gqo, gko, gvo, batch=batch, seq=seq, kv_heads=GQA_KV_HEADS,
                     units=tuple((0, r * LANE) for r in range(GQA_REP)),
                     tu=tq, tk=tk, sum_lane=HEAD_DIM, name="gqa_attn")
    oc = _flash_call(mqo, mko, mvo, batch=batch, seq=seq, kv_heads=MLA_HEADS,
                     units=tuple((u * tq, 0) for u in range(mla_units)),
                     tu=tq, tk=tk, sum_lane=MLA_V, name="mla_attn")
    x1, x1b, logits = _merge_call(
        x2, oa, ob, oc, prep["wg"], prep["wa"], prep["wb"], prep["wc"], prep["wo"],
        prep["ln1_g"], prep["ln1_b"], prep["wrh"], prep["wrl"], prep["br"], alpha=alpha, tm=tm)
    gate, dest, slot_token, block_expert, n_active = _routing(logits, T, tm=tm)
    xs = _gather_rows(x1b, slot_token)
    y = _moe_call(block_expert, n_active, xs, w_gate_up, bgu, w_down, bd, layer=layer)
    yk = _gather_rows(y, dest.T.reshape(-1)).reshape(TOP_K, T, D)
    return _final_call(x1, yk, gate, prep["ln2_g"], prep["ln2_b"], alpha=alpha, tm=tm)


def kernel(x, w_in, na_rpb, gqa_q_norm, gqa_k_norm, mla_q_norm, mla_kv_norm, w_uq, w_ukv, w_branch_a, w_branch_b, w_branch_c, w_out, ln1_g, ln1_b, w_router, b_router, w_gate_up, b_gate_up, w_down, b_down, ln2_g, ln2_b):
    B, S, D = x.shape
    depth = w_in.shape[0]
    alpha = (2.0 * depth) ** 0.25
    tables = _axial_tables(S)
    preps = []
    for l in range(depth):
        wqkv, gq, gk, wuq, wuk, wuv, wg, wb, wc = _layer_weights(
            w_in[l], gqa_q_norm[l], gqa_k_norm[l], w_uq[l], w_ukv[l], w_branch_b[l], w_branch_c[l])
        wr = jnp.pad(w_router[l], ((0, 0), (0, LANE - N_EXPERTS)))
        wrh = wr.astype(BF16)
        preps.append(dict(
            wqkv=wqkv, gq=gq, gk=gk, wuq=wuq, wuk=wuk, wuv=wuv, wg=wg, wb=wb, wc=wc,
            nq=mla_q_norm[l].reshape(1, -1), nkv=mla_kv_norm[l].reshape(1, -1),
            na_bias=_na_bias_table(na_rpb[l]), wa=w_branch_a[l].astype(BF16), wo=w_out[l].astype(BF16),
            ln1_g=ln1_g[l].reshape(1, D), ln1_b=ln1_b[l].reshape(1, D),
            ln2_g=ln2_g[l].reshape(1, D), ln2_b=ln2_b[l].reshape(1, D),
            wrh=wrh, wrl=jnp.concatenate([wrh, (wr - wrh.astype(F32)).astype(BF16)], axis=1),
            br=jnp.pad(b_router[l], (0, LANE - N_EXPERTS), constant_values=NEG_INF).reshape(1, LANE)))
    bgu = b_gate_up.reshape(depth, N_EXPERTS, 1, -1)
    bd = b_down.reshape(depth, N_EXPERTS, 1, -1)
    x2 = x.reshape(B * S, D)
    for l in range(depth):
        x2 = _layer(x2, preps[l], w_gate_up, bgu, w_down, bd, tables, layer=l, batch=B, seq=S, alpha=alpha)
    return x2.reshape(B, S, D)
```

```python
import functools
import math

import jax
import jax.numpy as jnp
from jax import lax
from jax.experimental import pallas as pl
from jax.experimental.pallas import tpu as pltpu

F32 = jnp.float32
BF16 = jnp.bfloat16

LANE = 128
GRID_W = 64
HEAD_DIM = 64
NA_HEADS = 6
NA_WIN_H = 8
NA_WIN_W = 16
NA_W = NA_HEADS * HEAD_DIM
GQA_HEADS = 6
GQA_KV_HEADS = 2
GQA_REP = GQA_HEADS // GQA_KV_HEADS
MLA_HEADS = 4
MLA_Q_RANK = 384
MLA_KV_RANK = 256
MLA_NOPE = 64
MLA_ROPE = 32
MLA_V = 64
MLA_QK = MLA_NOPE + MLA_ROPE
ROPE_THETA = 10000.0
N_BRANCH = 3
N_EXPERTS = 32
TOP_K = 4
D_EXPERT = 1024
SWIGLU_LIMIT = 7.0
SWIGLU_ALPHA = 1.702
MOE_BLOCK = 512
LN_EPS = 1e-5
RMS_EPS = 1e-6
NEG_INF = -1e30
LOG2_E = math.log2(math.e)
VMEM_LIMIT = 56 * 1024 * 1024
TOKEN_TILE = 512
QUERY_TILE = 512
KEY_CHUNK = 1024
MLA_STACKED_ROWS = 2048

_C_NAQ = 0
_C_NAK = _C_NAQ + NA_W
_C_NAV = _C_NAK + NA_W
_C_GQ = _C_NAV + NA_W
_C_GK = _C_GQ + GQA_HEADS * LANE
_C_GV = _C_GK + GQA_KV_HEADS * LANE
_C_CQ = _C_GV + GQA_KV_HEADS * LANE
_C_CKV = _C_CQ + MLA_Q_RANK
_C_KR = _C_CKV + MLA_KV_RANK
_C_END = _C_KR + LANE


def _cparams(sem):
    return pltpu.CompilerParams(dimension_semantics=sem, vmem_limit_bytes=VMEM_LIMIT)


def _dot(a, b):
    return jnp.dot(a, b, preferred_element_type=F32)


def _dot_nt(a, b):
    return lax.dot_general(a, b, (((1,), (1,)), ((), ())), preferred_element_type=F32)


def _qkv_kernel(x_ref, w_ref, cs_ref, mq_ref, mk_ref, gq_ref, gk_ref, nq_ref, nkv_ref,
                wuq_ref, wuk_ref, wuv_ref,
                naq_ref, nak_ref, nav_ref, gqo_ref, gko_ref, gvo_ref, mqo_ref, mko_ref, mvo_ref, h_scr):
    tm = x_ref.shape[0]
    h_scr[...] = _dot(x_ref[...].astype(BF16), w_ref[...])

    def proj(c0, width):
        return h_scr[:, c0:c0 + width]

    naq_ref[...] = proj(_C_NAQ, NA_W).astype(BF16)
    nak_ref[...] = proj(_C_NAK, NA_W).astype(BF16)
    nav_ref[...] = proj(_C_NAV, NA_W).astype(BF16)

    lane = lax.broadcasted_iota(jnp.int32, (tm, LANE), 1)
    cs = cs_ref[...]

    def norm_rope(hc, gain):
        r = lax.rsqrt(jnp.mean(hc * hc, axis=-1, keepdims=True) + RMS_EPS)
        a = hc * r * (gain * cs)
        return jnp.where(lane < HEAD_DIM, a + pltpu.roll(a, HEAD_DIM, 1), 0.0)

    gq_gain = gq_ref[...]
    for h in range(GQA_HEADS):
        hc = proj(_C_GQ + h * LANE, LANE)
        gqo_ref[:, h * LANE:(h + 1) * LANE] = norm_rope(hc, gq_gain).astype(BF16)
    gk_gain = gk_ref[...]
    for g in range(GQA_KV_HEADS):
        hc = proj(_C_GK + g * LANE, LANE)
        gko_ref[:, g * LANE:(g + 1) * LANE] = norm_rope(hc, gk_gain).astype(BF16)
        hv = proj(_C_GV + g * LANE, LANE)
        gvo_ref[:, g * LANE:(g + 1) * LANE] = jnp.where(lane == HEAD_DIM, 1.0, hv).astype(BF16)

    def rms(v, gain):
        r = lax.rsqrt(jnp.mean(v * v, axis=-1, keepdims=True) + RMS_EPS)
        return (v * r * gain).astype(BF16)

    def mla_rope(b):
        summed = b + pltpu.roll(b, LANE - MLA_ROPE, 1)
        return jnp.where(lane < MLA_NOPE, b, jnp.where(lane < MLA_QK, summed, 0.0))

    hq = _dot(rms(proj(_C_CQ, MLA_Q_RANK), nq_ref[...]), wuq_ref[...])
    mq = mq_ref[...]
    for h in range(MLA_HEADS):
        mqo_ref[:, h * LANE:(h + 1) * LANE] = mla_rope(hq[:, h * LANE:(h + 1) * LANE] * mq).astype(BF16)

    ckv = rms(proj(_C_CKV, MLA_KV_RANK), nkv_ref[...])
    hk = _dot(ckv, wuk_ref[...])
    hv = _dot(ckv, wuv_ref[...])
    kr = mla_rope(proj(_C_KR, LANE) * mk_ref[...])
    for h in range(MLA_HEADS):
        sl = slice(h * LANE, (h + 1) * LANE)
        mko_ref[:, sl] = (hk[:, sl] + kr).astype(BF16)
        mvo_ref[:, sl] = jnp.where(lane == MLA_V, 1.0, hv[:, sl]).astype(BF16)


def _qkv_call(x2, wqkv, cs, mq, mk, gq, gk, nq, nkv, wuq, wuk, wuv, *, seq, tm):
    T, D = x2.shape
    ns = seq // tm

    def rows(width):
        return pl.BlockSpec((tm, width), lambda i: (i, 0))

    def pos(width):
        return pl.BlockSpec((tm, width), lambda i: (i % ns, 0))

    def whole(a):
        return pl.BlockSpec(a.shape, lambda i: (0,) * a.ndim)

    widths = [NA_W, NA_W, NA_W, GQA_HEADS * LANE, GQA_KV_HEADS * LANE, GQA_KV_HEADS * LANE,
              MLA_HEADS * LANE, MLA_HEADS * LANE, MLA_HEADS * LANE]
    return pl.pallas_call(
        _qkv_kernel,
        grid=(T // tm,),
        in_specs=[rows(D), whole(wqkv), pos(LANE), pos(LANE), pos(LANE), whole(gq), whole(gk),
                  whole(nq), whole(nkv), whole(wuq), whole(wuk), whole(wuv)],
        out_specs=[rows(w) for w in widths],
        out_shape=[jax.ShapeDtypeStruct((T, w), BF16) for w in widths],
        scratch_shapes=[pltpu.VMEM((tm, _C_END), F32)],
        compiler_params=_cparams(("parallel",)),
        name="qkv_proj",
    )(x2, wqkv, cs, mq, mk, gq, gk, nq, nkv, wuq, wuk, wuv)


NA_ROWS_PER_STEP = 8
NA_ROWS_PER_ITER = 4


def _na_kernel(q_ref, k_ref, v_ref, bias_ref, o_ref, *, n_rows):
    j = pl.program_id(1)
    win = NA_WIN_H * GRID_W
    lane = lax.broadcasted_iota(jnp.int32, (GRID_W, LANE), 1)
    low = lane < HEAD_DIM

    def rows_body(a2, carry):
        work = []
        for rr in range(NA_ROWS_PER_ITER):
            a = a2 * NA_ROWS_PER_ITER + rr
            r = j * NA_ROWS_PER_STEP + a
            r0 = jnp.clip(r - NA_WIN_H // 2, 0, n_rows - NA_WIN_H)
            variant = r - r0
            qrow = pl.ds(pl.multiple_of(a * GRID_W, GRID_W), GRID_W)
            krow = pl.ds(pl.multiple_of(r0 * GRID_W, GRID_W), win)
            for pair in range(NA_HEADS // 2):
                cols = slice(pair * LANE, (pair + 1) * LANE)
                qp = q_ref[qrow, cols]
                kp = k_ref[krow, cols]
                for half in range(2):
                    qm = jnp.where(low if half == 0 else jnp.logical_not(low), qp, jnp.zeros_like(qp))
                    s = _dot_nt(qm, kp) + bias_ref[variant, 2 * pair + half]
                    work.append((qrow, krow, cols, half, s))
        probs = []
        for qrow, krow, cols, half, s in work:
            m = jnp.max(s, axis=-1, keepdims=True)
            p = jnp.exp(s - m)
            probs.append((p.astype(BF16), jnp.sum(p, axis=-1, keepdims=True)))
        outs = []
        for (qrow, krow, cols, half, _), (p, l) in zip(work, probs):
            outs.append(_dot(p, v_ref[krow, cols]) / l)
        for n in range(0, len(work), 2):
            qrow, _, cols, _, _ = work[n]
            o_ref[qrow, cols] = jnp.where(low, outs[n], outs[n + 1]).astype(BF16)
        return carry

    lax.fori_loop(0, NA_ROWS_PER_STEP // NA_ROWS_PER_ITER, rows_body, 0)


def _na_call(q, k, v, bias, *, batch, seq):
    n_rows = seq // GRID_W
    steps = n_rows // NA_ROWS_PER_STEP
    tq = NA_ROWS_PER_STEP * GRID_W
    return pl.pallas_call(
        functools.partial(_na_kernel, n_rows=n_rows),
        grid=(batch, steps),
        in_specs=[pl.BlockSpec((tq, NA_W), lambda b, j: (b * steps + j, 0)),
                  pl.BlockSpec((seq, NA_W), lambda b, j: (b, 0)),
                  pl.BlockSpec((seq, NA_W), lambda b, j: (b, 0)),
                  pl.BlockSpec(bias.shape, lambda b, j: (0, 0, 0, 0))],
        out_specs=pl.BlockSpec((tq, NA_W), lambda b, j: (b * steps + j, 0)),
        out_shape=jax.ShapeDtypeStruct(q.shape, BF16),
        compiler_params=_cparams(("parallel", "arbitrary")),
        name="na_attn",
    )(q, k, v, bias)


def _na_bias_table(rpb):
    cols = jnp.arange(GRID_W)
    c0 = jnp.clip(cols - NA_WIN_W // 2, 0, GRID_W - NA_WIN_W)
    in_win = (cols[None, :] >= c0[:, None]) & (cols[None, :] < c0[:, None] + NA_WIN_W)
    idx_c = jnp.clip(cols[None, :] - cols[:, None] + (NA_WIN_W - 1), 0, 2 * NA_WIN_W - 2)
    variant = jnp.arange(NA_WIN_H)
    idx_r = jnp.arange(NA_WIN_H)[None, :] - variant[:, None] + (NA_WIN_H - 1)
    b = rpb.astype(F32)[:, idx_r]
    b = b[..., idx_c]
    b = jnp.where(in_win[None, None, None], b, NEG_INF)
    b = b.transpose(1, 0, 3, 2, 4)
    return b.reshape(NA_WIN_H, NA_HEADS, GRID_W, NA_WIN_H * GRID_W)


def _flash_kernel(q_ref, k_ref, v_ref, o_ref, q_scr, s0, s1, p0, p1, a0, a1, m_scr, acc_scr,
                  *, units, tu, tk, nk, sum_lane):
    s_slot, p_slot, a_slot = (s0, s1), (p0, p1), (a0, a1)
    n_units = len(units)
    for u, (r0, c0) in enumerate(units):
        q_scr[u * tu:(u + 1) * tu, :] = q_ref[r0:r0 + tu, c0:c0 + LANE]
    m_scr[...] = jnp.full(m_scr.shape, NEG_INF, F32)
    acc_scr[...] = jnp.zeros(acc_scr.shape, F32)
    whole = (slice(0, n_units * tu),)
    per_unit = tuple(slice(u * tu, (u + 1) * tu) for u in range(n_units))

    def chunk(ref, c):
        start = c * tk if isinstance(c, int) else pl.multiple_of(c * tk, tk)
        return ref[pl.ds(start, tk), :]

    def scores(c, slot, parts):
        k = chunk(k_ref, c)
        for r in parts:
            s_slot[slot][r, :] = _dot_nt(q_scr[r, :], k)

    def softmax(slot, parts):
        for r in parts:
            s = s_slot[slot][r, :]
            m_prev = m_scr[r, :]
            m_new = jnp.maximum(m_prev, jnp.max(s, axis=-1, keepdims=True))
            a_slot[slot][r, :] = jnp.exp2(m_prev - m_new)
            p_slot[slot][r, :] = jnp.exp2(s - m_new[:, :1]).astype(BF16)
            m_scr[r, :] = m_new

    def accumulate(c, slot, parts):
        v = chunk(v_ref, c)
        for r in parts:
            acc_scr[r, :] = a_slot[slot][r, :] * acc_scr[r, :] + _dot(p_slot[slot][r, :], v)

    scores(0, 0, per_unit)
    scores(1, 1, per_unit)
    softmax(0, per_unit)

    def body(j, carry):
        c1 = 2 * j + 1
        scores(c1 + 1, 0, whole)
        accumulate(c1 - 1, 0, whole)
        softmax(1, whole)
        scores(c1 + 2, 1, whole)
        accumulate(c1, 1, whole)
        softmax(0, whole)
        return carry

    lax.fori_loop(0, (nk - 2) // 2, body, 0)
    accumulate(nk - 2, 0, per_unit)
    softmax(1, per_unit)
    accumulate(nk - 1, 1, per_unit)
    for u, (r0, c0) in enumerate(units):
        acc = acc_scr[u * tu:(u + 1) * tu, :]
        o_ref[r0:r0 + tu, c0:c0 + LANE] = (acc / acc[:, sum_lane:sum_lane + 1]).astype(BF16)


def _flash_call(q, k, v, *, batch, seq, kv_heads, units, tu, tk, sum_lane, name):
    rows = max(r0 for r0, _ in units) + tu
    width = max(c0 for _, c0 in units) + LANE
    nq = seq // rows
    nk = seq // tk
    assert nk >= 2 and nk % 2 == 0
    stacked = len(units) * tu
    return pl.pallas_call(
        functools.partial(_flash_kernel, units=units, tu=tu, tk=tk, nk=nk, sum_lane=sum_lane),
        grid=(batch, kv_heads, nq),
        in_specs=[pl.BlockSpec((rows, width), lambda b, g, i: (b * nq + i, g)),
                  pl.BlockSpec((seq, LANE), lambda b, g, i: (b, g)),
                  pl.BlockSpec((seq, LANE), lambda b, g, i: (b, g))],
        out_specs=pl.BlockSpec((rows, width), lambda b, g, i: (b * nq + i, g)),
        out_shape=jax.ShapeDtypeStruct(q.shape, BF16),
        scratch_shapes=[pltpu.VMEM((stacked, LANE), BF16),
                        pltpu.VMEM((stacked, tk), F32), pltpu.VMEM((stacked, tk), F32),
                        pltpu.VMEM((stacked, tk), BF16), pltpu.VMEM((stacked, tk), BF16),
                        pltpu.VMEM((stacked, LANE), F32), pltpu.VMEM((stacked, LANE), F32),
                        pltpu.VMEM((stacked, LANE), F32), pltpu.VMEM((stacked, LANE), F32)],
        compiler_params=_cparams(("parallel", "parallel", "arbitrary")),
        name=name,
    )(q, k, v)


def _layer_norm(z, g, b):
    mu = jnp.mean(z, axis=-1, keepdims=True)
    zc = z - mu
    var = jnp.mean(zc * zc, axis=-1, keepdims=True)
    return zc * lax.rsqrt(var + LN_EPS) * g + b


def _merge_kernel(x_ref, oa_ref, ob_ref, oc_ref, wg_ref, wa_ref, wb_ref, wc_ref, wo_ref,
                  lng_ref, lnb_ref, wrh_ref, wrl_ref, br_ref, x1_ref, x1b_ref, logit_ref, *, alpha):
    d = x_ref.shape[1]
    x = x_ref[...]
    xb = x.astype(BF16)
    mixed = None
    for i, (o_ref, w_ref) in enumerate(((oa_ref, wa_ref), (ob_ref, wb_ref), (oc_ref, wc_ref))):
        gate = jax.nn.sigmoid(_dot(xb, wg_ref[:, i * d:(i + 1) * d]))
        term = gate * _dot(o_ref[...], w_ref[...])
        mixed = term if mixed is None else mixed + term
    z = alpha * x + _dot(mixed.astype(BF16), wo_ref[...])
    x1 = _layer_norm(z, lng_ref[...], lnb_ref[...])
    x1_ref[...] = x1
    hi = x1.astype(BF16)
    lo = (x1 - hi.astype(F32)).astype(BF16)
    x1b_ref[...] = hi
    both = _dot(hi, wrl_ref[...])
    logit_ref[...] = both[:, :LANE] + both[:, LANE:] + _dot(lo, wrh_ref[...]) + br_ref[...]


def _merge_call(x2, oa, ob, oc, wg, wa, wb, wc, wo, lng, lnb, wrh, wrl, br, *, alpha, tm):
    T, D = x2.shape

    def rows(width):
        return pl.BlockSpec((tm, width), lambda i: (i, 0))

    def whole(a):
        return pl.BlockSpec(a.shape, lambda i: (0,) * a.ndim)

    return pl.pallas_call(
        functools.partial(_merge_kernel, alpha=alpha),
        grid=(T // tm,),
        in_specs=[rows(D), rows(oa.shape[1]), rows(ob.shape[1]), rows(oc.shape[1]),
                  whole(wg), whole(wa), whole(wb), whole(wc), whole(wo),
                  whole(lng), whole(lnb), whole(wrh), whole(wrl), whole(br)],
        out_specs=[rows(D), rows(D), rows(LANE)],
        out_shape=[jax.ShapeDtypeStruct((T, D), F32), jax.ShapeDtypeStruct((T, D), BF16),
                   jax.ShapeDtypeStruct((T, LANE), F32)],
        compiler_params=_cparams(("parallel",)),
        name="merge_ln_router",
    )(x2, oa, ob, oc, wg, wa, wb, wc, wo, lng, lnb, wrh, wrl, br)


def _cast_kernel(w_ref, o_ref):
    o_ref[...] = w_ref[...].astype(BF16)


def _cast_bf16(w):
    L, E, R, C = w.shape
    spec = pl.BlockSpec((1, 1, R, C), lambda n: (n // E, n % E, 0, 0))
    return pl.pallas_call(
        _cast_kernel,
        grid=(L * E,),
        in_specs=[spec],
        out_specs=spec,
        out_shape=jax.ShapeDtypeStruct(w.shape, BF16),
        compiler_params=_cparams(("parallel",)),
        name="cast_experts",
    )(w)


def _moe_kernel(be_ref, na_ref, xs_ref, wgu_ref, bgu_ref, wd_ref, bd_ref, y_ref):
    i = pl.program_id(0)

    @pl.when(i >= na_ref[0])
    def _():
        y_ref[...] = jnp.zeros(y_ref.shape, y_ref.dtype)

    @pl.when(i < na_ref[0])
    def _():
        h = _dot(xs_ref[...], wgu_ref[0, 0]) + bgu_ref[0, 0]
        g = jnp.minimum(h[:, :D_EXPERT], SWIGLU_LIMIT)
        u = jnp.clip(h[:, D_EXPERT:], -SWIGLU_LIMIT, SWIGLU_LIMIT)
        a = g * jax.nn.sigmoid(SWIGLU_ALPHA * g) * (u + 1.0)
        y_ref[...] = (_dot(a.astype(BF16), wd_ref[0, 0]) + bd_ref[0, 0]).astype(y_ref.dtype)


def _moe_call(block_expert, n_active, xs, wgu, bgu, wd, bd, *, layer):
    P, D = xs.shape
    n_blocks = P // MOE_BLOCK

    def blk(i, be, na):
        return (jnp.minimum(i, na[0] - 1), 0)

    def per_expert(i, be, na):
        return (layer, be[jnp.minimum(i, na[0] - 1)], 0, 0)

    grid_spec = pltpu.PrefetchScalarGridSpec(
        num_scalar_prefetch=2,
        grid=(n_blocks,),
        in_specs=[pl.BlockSpec((MOE_BLOCK, D), blk),
                  pl.BlockSpec((1, 1, D, 2 * D_EXPERT), per_expert),
                  pl.BlockSpec((1, 1, 1, 2 * D_EXPERT), per_expert),
                  pl.BlockSpec((1, 1, D_EXPERT, D), per_expert),
                  pl.BlockSpec((1, 1, 1, D), per_expert)],
        out_specs=pl.BlockSpec((MOE_BLOCK, D), lambda i, be, na: (i, 0)),
    )
    return pl.pallas_call(
        _moe_kernel,
        grid_spec=grid_spec,
        out_shape=jax.ShapeDtypeStruct((P, D), BF16),
        compiler_params=_cparams(("arbitrary",)),
        name="moe_ffn",
    )(block_expert, n_active, xs, wgu, bgu, wd, bd)


def _final_kernel(x_ref, y_ref, gate_ref, lng_ref, lnb_ref, o_ref, *, alpha):
    gate = gate_ref[...]
    f = None
    for k in range(TOP_K):
        term = y_ref[k].astype(F32) * gate[:, k:k + 1]
        f = term if f is None else f + term
    o_ref[...] = _layer_norm(alpha * x_ref[...] + f, lng_ref[...], lnb_ref[...])


def _final_call(x1, yk, gate, lng, lnb, *, alpha, tm):
    T, D = x1.shape
    return pl.pallas_call(
        functools.partial(_final_kernel, alpha=alpha),
        grid=(T // tm,),
        in_specs=[pl.BlockSpec((tm, D), lambda i: (i, 0)),
                  pl.BlockSpec((TOP_K, tm, D), lambda i: (0, i, 0)),
                  pl.BlockSpec((tm, TOP_K), lambda i: (i, 0)),
                  pl.BlockSpec((1, D), lambda i: (0, 0)),
                  pl.BlockSpec((1, D), lambda i: (0, 0))],
        out_specs=pl.BlockSpec((tm, D), lambda i: (i, 0)),
        out_shape=jax.ShapeDtypeStruct((T, D), F32),
        compiler_params=_cparams(("parallel",)),
        name="combine_ln",
    )(x1, yk, gate, lng, lnb)


def _rot_half(w):
    half = w.shape[-1] // 2
    return jnp.concatenate([w[..., half:], w[..., :half]], axis=-1)


def _axial_tables(seq):
    def cos_sin(dim):
        quarter = dim // 4
        inv = ROPE_THETA ** (-jnp.arange(quarter, dtype=F32) / quarter)
        t = jnp.arange(seq)
        row = (t // GRID_W).astype(F32)
        col = (t % GRID_W).astype(F32)
        ang = jnp.concatenate([row[:, None] * inv, col[:, None] * inv], -1)
        return jnp.cos(ang), jnp.sin(ang)

    c64, s64 = cos_sin(HEAD_DIM)
    c32, s32 = cos_sin(MLA_ROPE)
    cs = jnp.concatenate([c64, c64, -s64, s64], -1)
    m = jnp.concatenate([jnp.ones((seq, MLA_NOPE), F32), c32, c32, -s32, s32], -1)
    return cs, m * (MLA_QK ** -0.5 * LOG2_E), m


def _layer_weights(w_in, gqa_q_norm, gqa_k_norm, w_uq, w_ukv, w_branch_b, w_branch_c):
    D = w_in.shape[0]
    widths = [NA_W, NA_W, NA_W, GQA_HEADS * HEAD_DIM, GQA_KV_HEADS * HEAD_DIM, GQA_KV_HEADS * HEAD_DIM,
              MLA_Q_RANK, MLA_KV_RANK, MLA_ROPE, N_BRANCH * D]
    offs = [0]
    for w in widths:
        offs.append(offs[-1] + w)
    na_q, na_k, na_v, g_q, g_k, g_v, c_q, c_kv, k_r, gates = [w_in[:, offs[i]:offs[i + 1]] for i in range(10)]

    def heads_with_rot(w, n):
        w = w.reshape(D, n, HEAD_DIM)
        return jnp.concatenate([w, _rot_half(w)], -1).reshape(D, n * LANE)

    def heads_padded(w, n, width):
        w = w.reshape(w.shape[0], n, width)
        return jnp.pad(w, ((0, 0), (0, 0), (0, LANE - width))).reshape(w.shape[0], n * LANE)

    kr_cols = jnp.concatenate([jnp.zeros((D, MLA_NOPE), F32), k_r, _rot_half(k_r)], -1)
    wqkv = jnp.concatenate([na_q * (HEAD_DIM ** -0.5), na_k, na_v,
                            heads_with_rot(g_q, GQA_HEADS), heads_with_rot(g_k, GQA_KV_HEADS),
                            heads_padded(g_v, GQA_KV_HEADS, HEAD_DIM), c_q, c_kv, kr_cols], -1).astype(BF16)

    gq = (jnp.concatenate([gqa_q_norm, _rot_half(gqa_q_norm)]) * (HEAD_DIM ** -0.5 * LOG2_E)).reshape(1, LANE)
    gk = jnp.concatenate([gqa_k_norm, _rot_half(gqa_k_norm)]).reshape(1, LANE)

    uq = w_uq.reshape(MLA_Q_RANK, MLA_HEADS, MLA_QK)
    uq_rope = uq[..., MLA_NOPE:]
    wuq = jnp.concatenate([uq, _rot_half(uq_rope)], -1).reshape(MLA_Q_RANK, MLA_HEADS * LANE).astype(BF16)
    ukv = w_ukv.reshape(MLA_KV_RANK, MLA_HEADS, MLA_NOPE + MLA_V)
    wuk = heads_padded(ukv[..., :MLA_NOPE].reshape(MLA_KV_RANK, -1), MLA_HEADS, MLA_NOPE).astype(BF16)
    wuv = heads_padded(ukv[..., MLA_NOPE:].reshape(MLA_KV_RANK, -1), MLA_HEADS, MLA_V).astype(BF16)

    def rows_padded(w, n, width):
        w = w.reshape(n, width, D)
        return jnp.pad(w, ((0, 0), (0, LANE - width), (0, 0))).reshape(n * LANE, D).astype(BF16)

    wb = rows_padded(w_branch_b, GQA_HEADS, HEAD_DIM)
    wc = rows_padded(w_branch_c, MLA_HEADS, MLA_V)
    return wqkv, gq, gk, wuq, wuk, wuv, gates.astype(BF16), wb, wc


def _route_kernel(logit_ref, tril_ref, gate_ref, dest_ref, cnt_ref, cnt_scr, run_scr):
    phase = pl.program_id(0)
    i = pl.program_id(1)
    tm = logit_ref.shape[0]
    lane = lax.broadcasted_iota(jnp.int32, (tm, LANE), 1).astype(F32)

    @pl.when(jnp.logical_and(phase == 0, i == 0))
    def _():
        cnt_scr[...] = jnp.zeros(cnt_scr.shape, F32)
        run_scr[...] = jnp.zeros(run_scr.shape, F32)

    logits = logit_ref[...]
    sel = jnp.zeros((tm, LANE), F32)
    vals, hits = [], []
    for _ in range(TOP_K):
        m = jnp.max(logits, axis=-1, keepdims=True)
        idx = jnp.min(jnp.where(logits == m, lane, float(LANE)), axis=-1, keepdims=True)
        hit = lane == idx
        sel = jnp.where(hit, 1.0, sel)
        logits = jnp.where(hit, -jnp.inf, logits)
        vals.append(m)
        hits.append(hit)
    tile_total = jnp.sum(sel, axis=0, keepdims=True)

    @pl.when(phase == 0)
    def _():
        cnt_scr[...] = cnt_scr[...] + tile_total

    @pl.when(phase == 1)
    def _():
        incl = _dot(tril_ref[...], sel.astype(BF16))
        counts = cnt_scr[...]
        padded = jnp.floor((counts + (MOE_BLOCK - 1)) * (1.0 / MOE_BLOCK)) * MOE_BLOCK
        lane8 = lax.broadcasted_iota(jnp.int32, counts.shape, 1)
        cum = padded
        shift = 1
        while shift < LANE:
            cum = cum + jnp.where(lane8 >= shift, pltpu.roll(cum, shift, 1), 0.0)
            shift *= 2
        starts = (cum - padded) + run_scr[...]
        base = starts[0:1, :] + (incl - sel)
        es = [jnp.exp(v - vals[0]) for v in vals]
        denom = es[0] + es[1] + es[2] + es[3]
        gate_out = jnp.zeros((tm, LANE), F32)
        dest_out = jnp.zeros((tm, LANE), F32)
        for k in range(TOP_K):
            d_k = jnp.sum(jnp.where(hits[k], base, 0.0), axis=-1, keepdims=True)
            gate_out = jnp.where(lane == float(k), es[k] / denom, gate_out)
            dest_out = jnp.where(lane == float(k), d_k, dest_out)
        gate_ref[...] = gate_out
        dest_ref[...] = dest_out.astype(jnp.int32)
        run_scr[...] = run_scr[...] + tile_total
        cnt_ref[...] = counts[0:1, :]


def _route_call(logits, *, tm):
    T = logits.shape[0]
    nt = T // tm
    tril = (jnp.arange(tm)[:, None] >= jnp.arange(tm)[None, :]).astype(BF16)
    return pl.pallas_call(
        _route_kernel,
        grid=(2, nt),
        in_specs=[pl.BlockSpec((tm, LANE), lambda p, i: (i, 0)),
                  pl.BlockSpec((tm, tm), lambda p, i: (0, 0))],
        out_specs=[pl.BlockSpec((tm, LANE), lambda p, i: (p * i, 0)),
                   pl.BlockSpec((tm, LANE), lambda p, i: (p * i, 0)),
                   pl.BlockSpec((1, LANE), lambda p, i: (0, 0))],
        out_shape=[jax.ShapeDtypeStruct((T, LANE), F32), jax.ShapeDtypeStruct((T, LANE), jnp.int32),
                   jax.ShapeDtypeStruct((1, LANE), F32)],
        scratch_shapes=[pltpu.VMEM((8, LANE), F32), pltpu.VMEM((8, LANE), F32)],
        compiler_params=_cparams(("arbitrary", "arbitrary")),
        name="route",
    )(logits, tril)


def _routing(logits, n_tokens, *, tm):
    gate, dest, cnt = _route_call(logits, tm=tm)
    gate = gate[:, :TOP_K]
    dest = dest[:, :TOP_K]
    counts = cnt[0, :N_EXPERTS].astype(jnp.int32)
    padded = ((counts + MOE_BLOCK - 1) // MOE_BLOCK) * MOE_BLOCK
    cum_padded = jnp.cumsum(padded)
    starts_padded = cum_padded - padded
    starts_sorted = jnp.cumsum(counts) - counts
    n_blocks = n_tokens * TOP_K // MOE_BLOCK + N_EXPERTS
    block_start = jnp.arange(n_blocks, dtype=jnp.int32) * MOE_BLOCK
    block_expert = jnp.minimum(jnp.sum((block_start[:, None] >= cum_padded[None, :]).astype(jnp.int32), axis=1),
                               N_EXPERTS - 1)
    n_active = (cum_padded[-1] // MOE_BLOCK).astype(jnp.int32).reshape(1)
    tok = jnp.broadcast_to(jnp.arange(n_tokens, dtype=jnp.int32)[:, None], dest.shape)
    _, tok_sorted = lax.sort_key_val(dest.reshape(-1), tok.reshape(-1))
    per_expert = jnp.stack([starts_padded - starts_sorted, starts_padded, counts], axis=1)
    per_block = _gather_rows(per_expert, block_expert)
    offs = jnp.arange(MOE_BLOCK, dtype=jnp.int32)[None, :]
    slot = block_start[:, None] + offs
    src = jnp.clip(slot - per_block[:, 0:1], 0, n_tokens * TOP_K - 1)
    valid = (slot - per_block[:, 1:2]) < per_block[:, 2:3]
    slot_token = jnp.where(valid, _gather_rows(tok_sorted, src), slot % n_tokens).reshape(-1)
    return gate, dest, slot_token, block_expert, n_active


def _gather_rows(a, idx):
    return a.at[idx].get(mode="promise_in_bounds")


def _layer(x2, prep, w_gate_up, bgu, w_down, bd, tables, *, layer, batch, seq, alpha):
    T, D = x2.shape
    tm = min(TOKEN_TILE, seq)
    tq = min(QUERY_TILE, seq)
    tk = min(KEY_CHUNK, seq // 2)
    mla_units = max(1, min(MLA_STACKED_ROWS, seq) // tq)
    cs, mq_tab, mk_tab = tables
    naq, nak, nav, gqo, gko, gvo, mqo, mko, mvo = _qkv_call(
        x2, prep["wqkv"], cs, mq_tab, mk_tab, prep["gq"], prep["gk"], prep["nq"], prep["nkv"],
        prep["wuq"], prep["wuk"], prep["wuv"], seq=seq, tm=tm)
    oa = _na_call(naq, nak, nav, prep["na_bias"], batch=batch, seq=seq)
    ob = _flash_call(gqo, gko, gvo, batch=batch, seq=seq, kv_heads=GQA_KV_HEADS,
                     units=tuple((0, r * LANE) for r in range(GQA_REP)),
                     tu=tq, tk=tk, sum_lane=HEAD_DIM, name="gqa_attn")
    oc = _flash_call(mqo, mko, mvo, batch=batch, seq=seq, kv_heads=MLA_HEADS,
                     units=tuple((u * tq, 0) for u in range(mla_units)),
                     tu=tq, tk=tk, sum_lane=MLA_V, name="mla_attn")
    x1, x1b, logits = _merge_call(
        x2, oa, ob, oc, prep["wg"], prep["wa"], prep["wb"], prep["wc"], prep["wo"],
        prep["ln1_g"], prep["ln1_b"], prep["wrh"], prep["wrl"], prep["br"], alpha=alpha, tm=tm)
    gate, dest, slot_token, block_expert, n_active = _routing(logits, T, tm=tm)
    xs = _gather_rows(x1b, slot_token)
    y = _moe_call(block_expert, n_active, xs, w_gate_up, bgu, w_down, bd, layer=layer)
    yk = _gather_rows(y, dest.T.reshape(-1)).reshape(TOP_K, T, D)
    return _final_call(x1, yk, gate, prep["ln2_g"], prep["ln2_b"], alpha=alpha, tm=tm)


def kernel(x, w_in, na_rpb, gqa_q_norm, gqa_k_norm, mla_q_norm, mla_kv_norm, w_uq, w_ukv, w_branch_a, w_branch_b, w_branch_c, w_out, ln1_g, ln1_b, w_router, b_router, w_gate_up, b_gate_up, w_down, b_down, ln2_g, ln2_b):
    B, S, D = x.shape
    depth = w_in.shape[0]
    alpha = (2.0 * depth) ** 0.25
    tables = _axial_tables(S)
    preps = []
    for l in range(depth):
        wqkv, gq, gk, wuq, wuk, wuv, wg, wb, wc = _layer_weights(
            w_in[l], gqa_q_norm[l], gqa_k_norm[l], w_uq[l], w_ukv[l], w_branch_b[l], w_branch_c[l])
        wr = jnp.pad(w_router[l], ((0, 0), (0, LANE - N_EXPERTS)))
        wrh = wr.astype(BF16)
        preps.append(dict(
            wqkv=wqkv, gq=gq, gk=gk, wuq=wuq, wuk=wuk, wuv=wuv, wg=wg, wb=wb, wc=wc,
            nq=mla_q_norm[l].reshape(1, -1), nkv=mla_kv_norm[l].reshape(1, -1),
            na_bias=_na_bias_table(na_rpb[l]), wa=w_branch_a[l].astype(BF16), wo=w_out[l].astype(BF16),
            ln1_g=ln1_g[l].reshape(1, D), ln1_b=ln1_b[l].reshape(1, D),
            ln2_g=ln2_g[l].reshape(1, D), ln2_b=ln2_b[l].reshape(1, D),
            wrh=wrh, wrl=jnp.concatenate([wrh, (wr - wrh.astype(F32)).astype(BF16)], axis=1),
            br=jnp.pad(b_router[l], (0, LANE - N_EXPERTS), constant_values=NEG_INF).reshape(1, LANE)))
    bgu = b_gate_up.reshape(depth, N_EXPERTS, 1, -1)
    bd = b_down.reshape(depth, N_EXPERTS, 1, -1)
    wgu_bf = _cast_bf16(w_gate_up)
    wd_bf = _cast_bf16(w_down)
    x2 = x.reshape(B * S, D)
    for l in range(depth):
        x2 = _layer(x2, preps[l], wgu_bf, bgu, wd_bf, bd, tables, layer=l, batch=B, seq=S, alpha=alpha)
    return x2.reshape(B, S, D)
```
